```python
import jax
import jax.numpy as jnp
from jax import lax
import numpy as np

D_MODEL = 2048
BATCH = 4
SEQ = 4096
DEPTH = 4

GRID_W = 64
CTX_LEN = 256
HEAD_DIM = 128
A_HEADS = 8
A_KV = 2
A_WIN = 128
A_BLOCK = 128
B_HEADS = 8
B_WIN_ROWS = 8
B_WIN_COLS = 16
B_QCOLS = 16
C_WIDTH = 1024
CONV_W = 3
D_FF = 5632
N_MOD = 6
ROPE_THETA = 10000.0
EPS = 1e-6
NEG = -1e30

A_Q = A_HEADS * HEAD_DIM
A_KVW = A_KV * HEAD_DIM
B_W = B_HEADS * HEAD_DIM
SPLITS = (A_Q, A_KVW, A_KVW, B_W, B_W, B_W, C_WIDTH, C_WIDTH, C_WIDTH, D_MODEL, D_MODEL, D_MODEL)
IN_COLS = sum(SPLITS)

kernel_name = 'hybrid_latent_trunk'


def rmsnorm(x, g):
    xf = x.astype(jnp.float32)
    y = xf * lax.rsqrt(jnp.mean(xf * xf, axis=-1, keepdims=True) + EPS)
    return y.astype(x.dtype) * g


def modulate(h, shift, scale):
    return h * (1 + scale) + shift


def split_heads(t, n):
    return t.reshape(t.shape[:-1] + (n, t.shape[-1] // n))


def project(h, w):
    return jnp.split(h @ w, np.cumsum(SPLITS)[:-1].tolist(), axis=-1)


def dwconv3(u, w):
    return lax.conv_general_dilated(
        u, w[:, None, :].astype(u.dtype), window_strides=(1,),
        padding=((CONV_W // 2, CONV_W // 2),),
        dimension_numbers=('NWC', 'WIO', 'NWC'),
        feature_group_count=u.shape[-1])


def axial_rope(s_len):
    t = jnp.arange(s_len)
    row = (t // GRID_W).astype(jnp.float32)
    col = (t % GRID_W).astype(jnp.float32)
    n_freq = HEAD_DIM // 4
    inv = 1.0 / (ROPE_THETA ** (jnp.arange(n_freq, dtype=jnp.float32) / n_freq))
    ar = row[:, None] * inv
    ac = col[:, None] * inv
    ang = jnp.concatenate([ar, ar, ac, ac], axis=-1)
    return jnp.cos(ang), jnp.sin(ang)


def rotate_half(u):
    u1, u2 = jnp.split(u, 2, axis=-1)
    return jnp.concatenate([-u2, u1], axis=-1)


def apply_rope(u, cos, sin):
    uf = u.astype(jnp.float32)
    ur, uc = jnp.split(uf, 2, axis=-1)
    rot = jnp.concatenate([rotate_half(ur), rotate_half(uc)], axis=-1)
    return (uf * cos[:, None, :] + rot * sin[:, None, :]).astype(u.dtype)


def dense_attn(q, k, v, sink):
    bsz, t_len, n_h, hd = q.shape
    n_g = k.shape[2]
    rep = n_h // n_g
    qg = q.reshape(bsz, t_len, n_g, rep, hd)
    s = jnp.einsum('btgrd,bmgd->bgrtm', qg, k, preferred_element_type=jnp.float32) * hd ** -0.5
    if sink is not None:
        sk = jnp.broadcast_to(sink.astype(jnp.float32).reshape(1, n_g, rep, 1, 1), s.shape[:-1] + (1,))
        p = jax.nn.softmax(jnp.concatenate([s, sk], axis=-1), axis=-1)[..., :-1]
    else:
        p = jax.nn.softmax(s, axis=-1)
    o = jnp.einsum('bgrtm,bmgd->btgrd', p.astype(v.dtype), v)
    return o.reshape(bsz, t_len, n_h * hd)


def window_gqa(q, k, v, kc, vc, sink):
    bsz, s_len = q.shape[0], q.shape[1]
    nb = s_len // A_BLOCK
    rep = A_HEADS // A_KV
    qb = q.reshape(bsz, nb, A_BLOCK, A_KV, rep, HEAD_DIM)

    def band(u):
        up = jnp.pad(u.reshape(bsz, nb, A_BLOCK, A_KV, HEAD_DIM), ((0, 0), (1, 1), (0, 0), (0, 0), (0, 0)))
        return jnp.concatenate([up[:, :-2], up[:, 1:-1], up[:, 2:]], axis=2)

    kb, vb = band(k), band(v)
    scale = HEAD_DIM ** -0.5
    s_loc = jnp.einsum('bnqgrd,bnkgd->bngrqk', qb, kb, preferred_element_type=jnp.float32) * scale
    qpos = jnp.arange(A_BLOCK)[:, None]
    kpos = jnp.arange(3 * A_BLOCK)[None, :] - A_BLOCK
    kabs = jnp.arange(nb)[:, None] * A_BLOCK + kpos
    valid = (jnp.abs(kpos - qpos) <= A_WIN)[None] & ((kabs >= 0) & (kabs < s_len))[:, None, :]
    s_loc = jnp.where(valid[None, :, None, None], s_loc, NEG)
    s_ctx = jnp.einsum('bnqgrd,blgd->bngrql', qb, kc, preferred_element_type=jnp.float32) * scale
    s_sink = jnp.broadcast_to(sink.astype(jnp.float32).reshape(1, 1, A_KV, rep, 1, 1), s_loc.shape[:-1] + (1,))
    p = jax.nn.softmax(jnp.concatenate([s_loc, s_ctx, s_sink], axis=-1), axis=-1).astype(v.dtype)
    n_loc = 3 * A_BLOCK
    n_ctx = kc.shape[1]
    o = (jnp.einsum('bngrqk,bnkgd->bnqgrd', p[..., :n_loc], vb)
         + jnp.einsum('bngrql,blgd->bnqgrd', p[..., n_loc:n_loc + n_ctx], vc))
    return o.reshape(bsz, s_len, A_Q)


def neighbourhood_attn(q, k, v, kc, vc, rpb):
    bsz, s_len = q.shape[0], q.shape[1]
    rows = s_len // GRID_W
    kr = min(B_WIN_ROWS, rows)
    ncb = GRID_W // B_QCOLS
    span = B_QCOLS + B_WIN_COLS
    qcol = np.arange(GRID_W).reshape(ncb, B_QCOLS)
    cstart = np.clip(qcol - B_WIN_COLS // 2, 0, GRID_W - B_WIN_COLS)
    gstart = np.clip(np.arange(ncb) * B_QCOLS - B_WIN_COLS // 2, 0, GRID_W - span)
    kcol = gstart[:, None] + np.arange(span)
    rel = kcol[:, None, :] - qcol[:, :, None]
    col_ok = (kcol[:, None, :] >= cstart[:, :, None]) & (kcol[:, None, :] < cstart[:, :, None] + B_WIN_COLS)
    col_idx = np.clip(rel, 1 - B_WIN_COLS, B_WIN_COLS - 1) + B_WIN_COLS - 1
    col_ok = jnp.asarray(col_ok)[None, None, :, :, None, :]
    qg = q.reshape(bsz, rows, ncb, B_QCOLS, B_HEADS, HEAD_DIM)
    kg = k.reshape(bsz, rows, GRID_W, B_HEADS, HEAD_DIM)
    vg = v.reshape(bsz, rows, GRID_W, B_HEADS, HEAD_DIM)
    scale = HEAD_DIM ** -0.5
    n_loc = kr * span

    def one_row(r):
        rs = jnp.clip(r - kr // 2, 0, rows - kr)
        q_r = lax.dynamic_index_in_dim(qg, r, axis=1, keepdims=False)
        k_r = jnp.take(lax.dynamic_slice_in_dim(kg, rs, kr, axis=1), kcol, axis=2)
        v_r = jnp.take(lax.dynamic_slice_in_dim(vg, rs, kr, axis=1), kcol, axis=2)
        row_idx = rs + jnp.arange(kr) - r + B_WIN_ROWS - 1
        bias = jnp.take(rpb, row_idx, axis=1)[:, :, col_idx]
        bias = jnp.transpose(bias, (0, 2, 3, 1, 4)).astype(jnp.float32)
        s_loc = jnp.einsum('bjqhd,brjkhd->bhjqrk', q_r, k_r, preferred_element_type=jnp.float32) * scale + bias[None]
        s_loc = jnp.where(col_ok, s_loc, NEG).reshape(bsz, B_HEADS, ncb, B_QCOLS, n_loc)
        s_ctx = jnp.einsum('bjqhd,blhd->bhjql', q_r, kc, preferred_element_type=jnp.float32) * scale
        p = jax.nn.softmax(jnp.concatenate([s_loc, s_ctx], axis=-1), axis=-1).astype(v.dtype)
        p_loc = p[..., :n_loc].reshape(bsz, B_HEADS, ncb, B_QCOLS, kr, span)
        o = (jnp.einsum('bhjqrk,brjkhd->bjqhd', p_loc, v_r)
             + jnp.einsum('bhjql,blhd->bjqhd', p[..., n_loc:], vc))
        return o.reshape(bsz, GRID_W, B_W)

    out = lax.map(one_row, jnp.arange(rows))
    return jnp.moveaxis(out, 0, 1).reshape(bsz, s_len, B_W)


def short_conv(u, g_pre, g_post, w):
    return g_post * dwconv3(g_pre * u, w)


def merge(a, b, cb, za, zb, zc, w_pa, w_pb, w_pc, w_o):
    m = (jax.nn.sigmoid(za) * (a @ w_pa) + jax.nn.sigmoid(zb) * (b @ w_pb)
         + jax.nn.sigmoid(zc) * (cb @ w_pc))
    return m @ w_o


def conv_ffn(h, w_up, conv_f, w_down):
    gate, val = jnp.split(dwconv3(h @ w_up, conv_f), 2, axis=-1)
    return (jax.nn.silu(gate) * val) @ w_down


def setup_inputs(seed: int = 0) -> dict:
    key = jax.random.key(seed)
    ks = jax.random.split(key, 24)

    def nrm(k, shape, scale):
        return jax.random.normal(k, shape, jnp.float32) * scale

    L = DEPTH
    return {
        'x': nrm(ks[0], (BATCH, SEQ, D_MODEL), 1.0),
        'c': nrm(ks[1], (BATCH, D_MODEL), 1.0),
        'ctx': nrm(ks[2], (BATCH, CTX_LEN, D_MODEL), 1.0),
        'c_ctx': nrm(ks[3], (D_MODEL,), 1.0),
        'w_mod': nrm(ks[4], (L, D_MODEL, N_MOD * D_MODEL), D_MODEL ** -0.5),
        'b_mod': nrm(ks[5], (L, N_MOD * D_MODEL), 0.02),
        'norm1': 1.0 + nrm(ks[6], (L, D_MODEL), 0.02),
        'w_in': nrm(ks[7], (L, D_MODEL, IN_COLS), D_MODEL ** -0.5),
        'sink': nrm(ks[8], (L, A_HEADS), 0.5),
        'rpb': nrm(ks[9], (L, B_HEADS, 2 * B_WIN_ROWS - 1, 2 * B_WIN_COLS - 1), 0.5),
        'conv_c': nrm(ks[10], (L, CONV_W, C_WIDTH), CONV_W ** -0.5),
        'w_pa': nrm(ks[11], (L, A_Q, D_MODEL), A_Q ** -0.5),
        'w_pb': nrm(ks[12], (L, B_W, D_MODEL), B_W ** -0.5),
        'w_pc': nrm(ks[13], (L, C_WIDTH, D_MODEL), C_WIDTH ** -0.5),
        'w_o': nrm(ks[14], (L, D_MODEL, D_MODEL), D_MODEL ** -0.5),
        'norm2': 1.0 + nrm(ks[15], (L, D_MODEL), 0.02),
        'w_up': nrm(ks[16], (L, D_MODEL, 2 * D_FF), D_MODEL ** -0.5),
        'conv_f': nrm(ks[17], (L, CONV_W, 2 * D_FF), CONV_W ** -0.5),
        'w_down': nrm(ks[18], (L, D_FF, D_MODEL), D_FF ** -0.5),
        'norm_f': 1.0 + nrm(ks[19], (D_MODEL,), 0.02),
    }


def reference(x, c, ctx, c_ctx, w_mod, b_mod, norm1, w_in, sink, rpb, conv_c,
              w_pa, w_pb, w_pc, w_o, norm2, w_up, conv_f, w_down, norm_f):
    s_len = x.shape[1]
    cos, sin = axial_rope(s_len)
    silu_c = jax.nn.silu(c)
    silu_cc = jax.nn.silu(c_ctx)
    xc = ctx
    for l in range(DEPTH):
        mx = jnp.split((silu_c @ w_mod[l] + b_mod[l])[:, None, :], N_MOD, axis=-1)
        mc = jnp.split(silu_cc @ w_mod[l] + b_mod[l], N_MOD, axis=-1)
        hx = modulate(rmsnorm(x, norm1[l]), mx[0], mx[1])
        hc = modulate(rmsnorm(xc, norm1[l]), mc[0], mc[1])
        qa, ka, va, qb, kb, vb, uc, gpre, gpost, za, zb, zc = project(hx, w_in[l])
        qa_c, ka_c, va_c, qb_c, kb_c, vb_c, uc_c, gpre_c, gpost_c, za_c, zb_c, zc_c = project(hc, w_in[l])
        ka_ctx, va_ctx = split_heads(ka_c, A_KV), split_heads(va_c, A_KV)
        kb_ctx, vb_ctx = split_heads(kb_c, B_HEADS), split_heads(vb_c, B_HEADS)
        a = window_gqa(apply_rope(split_heads(qa, A_HEADS), cos, sin),
                       apply_rope(split_heads(ka, A_KV), cos, sin),
                       split_heads(va, A_KV), ka_ctx, va_ctx, sink[l])
        b = neighbourhood_attn(split_heads(qb, B_HEADS), split_heads(kb, B_HEADS),
                               split_heads(vb, B_HEADS), kb_ctx, vb_ctx, rpb[l])
        cb = short_conv(uc, gpre, gpost, conv_c[l])
        x = x + mx[2] * merge(a, b, cb, za, zb, zc, w_pa[l], w_pb[l], w_pc[l], w_o[l])
        if l < DEPTH - 1:
            a_c = dense_attn(split_heads(qa_c, A_HEADS), ka_ctx, va_ctx, sink[l])
            b_c = dense_attn(split_heads(qb_c, B_HEADS), kb_ctx, vb_ctx, None)
            cb_c = short_conv(uc_c, gpre_c, gpost_c, conv_c[l])
            xc = xc + mc[2] * merge(a_c, b_c, cb_c, za_c, zb_c, zc_c, w_pa[l], w_pb[l], w_pc[l], w_o[l])
        x = x + mx[5] * conv_ffn(modulate(rmsnorm(x, norm2[l]), mx[3], mx[4]), w_up[l], conv_f[l], w_down[l])
        if l < DEPTH - 1:
            xc = xc + mc[5] * conv_ffn(modulate(rmsnorm(xc, norm2[l]), mc[3], mc[4]), w_up[l], conv_f[l], w_down[l])
    return rmsnorm(x, norm_f)
```

```python
import functools

import numpy as np
import jax
import jax.numpy as jnp
from jax import lax
from jax.experimental import pallas as pl
from jax.experimental.pallas import tpu as pltpu

F32 = jnp.float32
BF16 = jnp.bfloat16

D = 2048
NB = 4
S = 4096
DEPTH = 4
GRID_W = 64
CTX = 256
HD = 128
A_HEADS = 8
A_KV = 2
A_REP = A_HEADS // A_KV
A_WIN = 128
A_BLOCK = 128
B_HEADS = 8
B_WIN_ROWS = 8
B_WIN_COLS = 16
C_WIDTH = 1024
D_FF = 5632
N_MOD = 6
ROPE_THETA = 10000.0
EPS = 1e-6
NEG = -1e30
SCALE = HD ** -0.5

T_LAT = NB * S
T_CTX = NB * CTX
T_ALL = T_LAT + T_CTX

OFF_QA = 0
OFF_QB = 1024
OFF_KB = 2048
OFF_UC = 3072
OFF_ZA = 6144
OFF_ZB = 8192
OFF_ZC = 10240
OFF_VB = 12288
OFF_KA = 13312
OFF_VA = 13568
IN_COLS = 13824

TM = 512
TN = 512
LAT_TILES = T_LAT // TM
ALL_TILES = T_ALL // TM
TILES_PER_SEQ = S // TM
HALO = 8

VMEM_LIMIT = 56 * 1024 * 1024


def _cparams(*sem):
    return pltpu.CompilerParams(dimension_semantics=sem, vmem_limit_bytes=VMEM_LIMIT)


def _mod_row(i):
    return jnp.minimum(i // TILES_PER_SEQ, NB)


def _mod_spec(l, k, ncols=None):
    if ncols is None:
        return pl.BlockSpec((None, None, None, 1, D), lambda i, j: (l, _mod_row(i), k, 0, 0))
    return pl.BlockSpec((None, None, None, 1, ncols), lambda i, j: (l, _mod_row(i), k, 0, j))


def _mods_kernel(c_ref, w_ref, b_ref, o_ref):
    cv = c_ref[...]
    sc = cv * jax.nn.sigmoid(cv)
    o_ref[...] = jnp.dot(sc, w_ref[...], preferred_element_type=F32,
                         precision=lax.Precision.HIGHEST) + b_ref[...]


def _mods(c8, w_mod, b_mod):
    tn = 1024
    ncols = N_MOD * D
    return pl.pallas_call(
        _mods_kernel,
        grid=(DEPTH, ncols // tn),
        in_specs=[
            pl.BlockSpec((8, D), lambda l, j: (0, 0)),
            pl.BlockSpec((None, D, tn), lambda l, j: (l, 0, j)),
            pl.BlockSpec((None, 1, tn), lambda l, j: (l, 0, j)),
        ],
        out_specs=pl.BlockSpec((None, 8, tn), lambda l, j: (l, 0, j)),
        out_shape=jax.ShapeDtypeStruct((DEPTH, 8, ncols), F32),
        compiler_params=_cparams("arbitrary", "arbitrary"),
        name="mods",
    )(c8, w_mod, b_mod.reshape(DEPTH, 1, ncols))


def _norm_mod(x, g, shift, scale):
    ms = jnp.mean(x * x, axis=-1, keepdims=True)
    y = x * lax.rsqrt(ms + EPS)
    y = y * g
    return y * (1.0 + scale) + shift


def _rope(u, cos, sin_signed, swap_lo):
    rot = jnp.where(swap_lo, pltpu.roll(u, 96, axis=1), pltpu.roll(u, 32, axis=1))
    return u * cos + rot * sin_signed


def _inproj_kernel(x_ref, g_ref, shift_ref, scale_ref, w_ref, cos_ref, sin_ref, o_ref, h_ref,
                   *, lat_tiles):
    i = pl.program_id(0)
    j = pl.program_id(1)

    @pl.when(j == 0)
    def _():
        h = _norm_mod(x_ref[...], g_ref[...], shift_ref[...], scale_ref[...])
        h_ref[...] = h.astype(BF16)

    acc = jnp.dot(h_ref[...], w_ref[...], preferred_element_type=F32)
    is_lat = i < lat_tiles
    qa_tile = jnp.logical_and(is_lat, j < (OFF_QB // TN))
    ka_tile = jnp.logical_and(is_lat, j == (OFF_KA // TN))

    def roped(chunk):
        lane = lax.broadcasted_iota(jnp.int32, (TM, HD), 1)
        swap_lo = (lane & 32) == 0
        return _rope(acc[:, chunk * HD:(chunk + 1) * HD], cos_ref[...], sin_ref[...], swap_lo)

    @pl.when(qa_tile)
    def _():
        for ch in range(TN // HD):
            o_ref[:, ch * HD:(ch + 1) * HD] = roped(ch).astype(BF16)

    @pl.when(ka_tile)
    def _():
        n_k = (OFF_VA - OFF_KA) // HD
        for ch in range(n_k):
            o_ref[:, ch * HD:(ch + 1) * HD] = roped(ch).astype(BF16)
        o_ref[:, n_k * HD:] = acc[:, n_k * HD:].astype(BF16)

    @pl.when(jnp.logical_not(jnp.logical_or(qa_tile, ka_tile)))
    def _():
        o_ref[...] = acc.astype(BF16)


def _inproj(l, xall, norm1, mods5, w_in, cos, sin_signed):
    kern = functools.partial(_inproj_kernel, lat_tiles=LAT_TILES)
    return pl.pallas_call(
        kern,
        grid=(ALL_TILES, IN_COLS // TN),
        in_specs=[
            pl.BlockSpec((TM, D), lambda i, j: (i, 0)),
            pl.BlockSpec((None, 1, D), lambda i, j: (l, 0, 0)),
            _mod_spec(l, 0),
            _mod_spec(l, 1),
            pl.BlockSpec((None, D, TN), lambda i, j: (l, 0, j)),
            pl.BlockSpec((TM, HD), lambda i, j: (i % TILES_PER_SEQ, 0)),
            pl.BlockSpec((TM, HD), lambda i, j: (i % TILES_PER_SEQ, 0)),
        ],
        out_specs=pl.BlockSpec((TM, TN), lambda i, j: (i, j)),
        out_shape=jax.ShapeDtypeStruct((T_ALL, IN_COLS), BF16),
        scratch_shapes=[pltpu.VMEM((TM, D), BF16)],
        compiler_params=_cparams("arbitrary", "arbitrary"),
        name="inproj",
    )(xall, norm1.reshape(DEPTH, 1, D), mods5, mods5, w_in, cos, sin_signed)


A_KEYS = 3 * A_BLOCK


def _stack_heads(q, rep):
    if rep == 1:
        return q
    return jnp.concatenate([q[:, r * HD:(r + 1) * HD] for r in range(rep)], axis=0)


def _dot_nt(a, b):
    return lax.dot_general(a, b, (((1,), (1,)), ((), ())), preferred_element_type=F32)


def _attn_a_kernel(sink_ref, q_ref, k_ref, v_ref, kc_ref, vc_ref, o_ref):
    g = pl.program_id(1)
    n = pl.program_id(2)
    q4 = _stack_heads(q_ref[...], A_REP)
    start = pl.multiple_of(jnp.clip(n * A_BLOCK - A_BLOCK, 0, S - A_KEYS), A_BLOCK)
    k = k_ref[pl.ds(start, A_KEYS), :]
    v = v_ref[pl.ds(start, A_KEYS), :]
    rows = A_REP * A_BLOCK
    s_loc = _dot_nt(q4, k) * SCALE
    s_ctx = _dot_nt(q4, kc_ref[...]) * SCALE
    qpos = n * A_BLOCK + (lax.broadcasted_iota(jnp.int32, (rows, A_KEYS), 0) & (A_BLOCK - 1))
    kpos = start + lax.broadcasted_iota(jnp.int32, (rows, A_KEYS), 1)
    s_loc = jnp.where(jnp.abs(kpos - qpos) <= A_WIN, s_loc, NEG)
    ridx = lax.broadcasted_iota(jnp.int32, (rows, 1), 0)
    sink = jnp.full((rows, 1), sink_ref[g * A_REP], F32)
    for r in range(1, A_REP):
        sink = jnp.where(ridx >= r * A_BLOCK, sink_ref[g * A_REP + r], sink)
    m = jnp.maximum(jnp.maximum(jnp.max(s_loc, axis=1, keepdims=True),
                                jnp.max(s_ctx, axis=1, keepdims=True)), sink)
    e_loc = jnp.exp(s_loc - m)
    e_ctx = jnp.exp(s_ctx - m)
    denom = (jnp.sum(e_loc, axis=1, keepdims=True) + jnp.sum(e_ctx, axis=1, keepdims=True)
             + jnp.exp(sink - m))
    o = (jnp.dot(e_loc.astype(BF16), v, preferred_element_type=F32)
         + jnp.dot(e_ctx.astype(BF16), vc_ref[...], preferred_element_type=F32))
    o = o / denom
    for r in range(A_REP):
        o_ref[:, r * HD:(r + 1) * HD] = o[r * A_BLOCK:(r + 1) * A_BLOCK, :].astype(BF16)


def _attn_a(p, sink_l):
    nblk = S // A_BLOCK
    qw = A_REP * HD
    return pl.pallas_call(
        _attn_a_kernel,
        grid_spec=pltpu.PrefetchScalarGridSpec(
            num_scalar_prefetch=1,
            grid=(NB, A_KV, nblk),
            in_specs=[
                pl.BlockSpec((A_BLOCK, qw), lambda b, g, n, s: (b * nblk + n, g)),
                pl.BlockSpec((S, HD), lambda b, g, n, s: (b, OFF_KA // HD + g)),
                pl.BlockSpec((S, HD), lambda b, g, n, s: (b, OFF_VA // HD + g)),
                pl.BlockSpec((CTX, HD), lambda b, g, n, s: (T_LAT // CTX + b, OFF_KA // HD + g)),
                pl.BlockSpec((CTX, HD), lambda b, g, n, s: (T_LAT // CTX + b, OFF_VA // HD + g)),
            ],
            out_specs=pl.BlockSpec((A_BLOCK, qw), lambda b, g, n, s: (b * nblk + n, g)),
        ),
        out_shape=jax.ShapeDtypeStruct((T_ALL, A_HEADS * HD), BF16),
        compiler_params=_cparams("arbitrary", "arbitrary", "arbitrary"),
        name="attn_a",
    )(sink_l, p, p, p, p, p)


B_GROUP = 8
B_Q = B_GROUP * GRID_W
B_KROWS = 16
B_K = B_KROWS * GRID_W
B_NTAB = 2 * B_WIN_ROWS - 2


def _bias_tables(rpb):
    c = np.arange(GRID_W)[:, None]
    kc = np.arange(GRID_W)[None, :]
    cs = np.clip(c - B_WIN_COLS // 2, 0, GRID_W - B_WIN_COLS)
    ok = (kc >= cs) & (kc < cs + B_WIN_COLS)
    idx = np.clip(kc - c, 1 - B_WIN_COLS, B_WIN_COLS - 1) + B_WIN_COLS - 1
    mt = jnp.where(jnp.asarray(ok), rpb[..., idx].astype(F32), NEG)
    return jnp.concatenate([mt[:, :, :-1], mt[:, :, 1:]], axis=-1)


def _attn_b_kernel(q_ref, k_ref, v_ref, kc_ref, vc_ref, t_ref, o_ref, s_ref, p_ref, pc_ref, inv_ref):
    g = pl.program_id(2)
    n_groups = (S // GRID_W) // B_GROUP
    r0 = g * B_GROUP
    base = jnp.clip(r0 - B_WIN_ROWS // 2, 0, S // GRID_W - B_KROWS)
    kstart = pl.multiple_of(base * GRID_W, 4 * GRID_W)
    q = q_ref[...]
    k = k_ref[pl.ds(kstart, B_K), :]
    v = v_ref[pl.ds(kstart, B_K), :]
    s_ref[...] = _dot_nt(q, k)
    sc_all = _dot_nt(q, kc_ref[...]) * SCALE
    p_ref[...] = jnp.zeros_like(p_ref)
    lane = lax.broadcasted_iota(jnp.int32, (GRID_W, 2 * GRID_W), 1)
    left = lane < GRID_W

    def body(off, jlo_of):
        for i in range(B_GROUP):
            jlo = jlo_of(i)
            p0, p1 = jlo // 2, (jlo + B_WIN_ROWS - 1) // 2
            rows = slice(i * GRID_W, (i + 1) * GRID_W)
            cols = slice(2 * GRID_W * p0, 2 * GRID_W * (p1 + 1))
            tiles = []
            for pp in range(p0, p1 + 1):
                t = t_ref[off + 2 * pp - i + B_WIN_ROWS - 1]
                if 2 * pp < jlo:
                    t = jnp.where(left, NEG, t)
                if 2 * pp + 1 >= jlo + B_WIN_ROWS:
                    t = jnp.where(left, t, NEG)
                tiles.append(t)
            sl = s_ref[rows, cols] * SCALE + jnp.concatenate(tiles, axis=1)
            sc = sc_all[rows, :]
            m = jnp.maximum(jnp.max(sl, axis=1, keepdims=True), jnp.max(sc, axis=1, keepdims=True))
            e = jnp.exp(sl - m)
            ec = jnp.exp(sc - m)
            denom = jnp.sum(e, axis=1, keepdims=True) + jnp.sum(ec, axis=1, keepdims=True)
            p_ref[rows, cols] = e.astype(BF16)
            pc_ref[rows, :] = ec.astype(BF16)
            inv_ref[rows, :] = 1.0 / denom

    half = B_WIN_ROWS // 2

    @pl.when(g == 0)
    def _():
        body(0, lambda i: max(i - half, 0))

    @pl.when(jnp.logical_and(g > 0, g < n_groups - 1))
    def _():
        body(-half, lambda i: i)

    @pl.when(g == n_groups - 1)
    def _():
        body(-(B_KROWS - B_GROUP), lambda i: min(i + half, B_KROWS - B_WIN_ROWS))

    o = (jnp.dot(p_ref[...], v, preferred_element_type=F32)
         + jnp.dot(pc_ref[...], vc_ref[...], preferred_element_type=F32))
    o_ref[...] = (o * inv_ref[...]).astype(BF16)


def _attn_b(l, p, tables):
    n_groups = (S // GRID_W) // B_GROUP
    return pl.pallas_call(
        _attn_b_kernel,
        grid=(B_HEADS, NB, n_groups),
        in_specs=[
            pl.BlockSpec((B_Q, HD), lambda h, b, g: (b * n_groups + g, OFF_QB // HD + h)),
            pl.BlockSpec((S, HD), lambda h, b, g: (b, OFF_KB // HD + h)),
            pl.BlockSpec((S, HD), lambda h, b, g: (b, OFF_VB // HD + h)),
            pl.BlockSpec((CTX, HD), lambda h, b, g: (T_LAT // CTX + b, OFF_KB // HD + h)),
            pl.BlockSpec((CTX, HD), lambda h, b, g: (T_LAT // CTX + b, OFF_VB // HD + h)),
            pl.BlockSpec((None, None, B_NTAB, GRID_W, 2 * GRID_W), lambda h, b, g: (l, h, 0, 0, 0)),
        ],
        out_specs=pl.BlockSpec((B_Q, HD), lambda h, b, g: (b * n_groups + g, h)),
        out_shape=jax.ShapeDtypeStruct((T_ALL, B_HEADS * HD), BF16),
        scratch_shapes=[
            pltpu.VMEM((B_Q, B_K), F32),
            pltpu.VMEM((B_Q, B_K), BF16),
            pltpu.VMEM((B_Q, CTX), BF16),
            pltpu.VMEM((B_Q, 1), F32),
        ],
        compiler_params=_cparams("arbitrary", "arbitrary", "arbitrary"),
        name="attn_b",
    )(p, p, p, p, p, tables)


def _ctx_attn_kernel(sink_ref, q_ref, k_ref, v_ref, prev_ref, o_ref, *, rep, has_sink):
    del prev_ref
    g = pl.program_id(1)
    qs = _stack_heads(q_ref[...], rep)
    s = _dot_nt(qs, k_ref[...]) * SCALE
    m = jnp.max(s, axis=1, keepdims=True)
    rows = rep * CTX
    if has_sink:
        ridx = lax.broadcasted_iota(jnp.int32, (rows, 1), 0)
        sink = jnp.full((rows, 1), sink_ref[g * rep], F32)
        for r in range(1, rep):
            sink = jnp.where(ridx >= r * CTX, sink_ref[g * rep + r], sink)
        m = jnp.maximum(m, sink)
    e = jnp.exp(s - m)
    denom = jnp.sum(e, axis=1, keepdims=True)
    if has_sink:
        denom = denom + jnp.exp(sink - m)
    o = jnp.dot(e.astype(BF16), v_ref[...], preferred_element_type=F32) / denom
    for r in range(rep):
        o_ref[:, r * HD:(r + 1) * HD] = o[r * CTX:(r + 1) * CTX, :].astype(BF16)


def _ctx_attn(p, sink_l, prev, *, off_q, off_k, off_v, n_kv, rep, has_sink):
    qw = rep * HD
    rb = T_LAT // CTX
    kern = functools.partial(_ctx_attn_kernel, rep=rep, has_sink=has_sink)
    return pl.pallas_call(
        kern,
        grid_spec=pltpu.PrefetchScalarGridSpec(
            num_scalar_prefetch=1,
            grid=(NB, n_kv),
            in_specs=[
                pl.BlockSpec((CTX, qw), lambda b, g, s: (rb + b, off_q // qw + g)),
                pl.BlockSpec((CTX, HD), lambda b, g, s: (rb + b, off_k // HD + g)),
                pl.BlockSpec((CTX, HD), lambda b, g, s: (rb + b, off_v // HD + g)),
                pl.BlockSpec(memory_space=pl.ANY),
            ],
            out_specs=pl.BlockSpec((CTX, qw), lambda b, g, s: (rb + b, g)),
        ),
        out_shape=jax.ShapeDtypeStruct(prev.shape, prev.dtype),
        input_output_aliases={4: 0},
        compiler_params=_cparams("arbitrary", "arbitrary"),
        name="ctx_attn_a" if has_sink else "ctx_attn_b",
    )(sink_l, p, p, p, prev)


def _seq_masks(i, shape):
    lmask = jnp.where(i < LAT_TILES, S - 1, CTX - 1)
    pos = (i * TM + lax.broadcasted_iota(jnp.int32, shape, 0)) & lmask
    return pos != 0, pos != lmask


def _merge_kernel(a_ref, b_ref, c_ref, cp_ref, cn_ref, za_ref, zb_ref, zc_ref,
                  wa_ref, wb_ref, wc_ref, cw_ref, o_ref, cb_ref):
    i = pl.program_id(0)
    j = pl.program_id(1)

    @pl.when(j == 0)
    def _():
        w = C_WIDTH

        def gated(ref):
            return ref[:, w:2 * w].astype(F32) * ref[:, 0:w].astype(F32)

        v = gated(c_ref)
        vp = gated(cp_ref)[HALO - 1:HALO, :]
        vn = gated(cn_ref)[0:1, :]
        row = lax.broadcasted_iota(jnp.int32, (TM, w), 0)
        v_prev = jnp.where(row == 0, vp, pltpu.roll(v, 1, axis=0))
        v_next = jnp.where(row == TM - 1, vn, pltpu.roll(v, TM - 1, axis=0))
        has_prev, has_next = _seq_masks(i, (TM, w))
        cw = cw_ref[...]
        conv = (cw[0:1, :] * jnp.where(has_prev, v_prev, 0.0) + cw[1:2, :] * v
                + cw[2:3, :] * jnp.where(has_next, v_next, 0.0))
        cb_ref[...] = (c_ref[:, 2 * w:3 * w].astype(F32) * conv).astype(BF16)

    ya = jnp.dot(a_ref[...], wa_ref[...], preferred_element_type=F32)
    yb = jnp.dot(b_ref[...], wb_ref[...], preferred_element_type=F32)
    yc = jnp.dot(cb_ref[...], wc_ref[...], preferred_element_type=F32)
    m = (jax.nn.sigmoid(za_ref[...].astype(F32)) * ya + jax.nn.sigmoid(zb_ref[...].astype(F32)) * yb
         + jax.nn.sigmoid(zc_ref[...].astype(F32)) * yc)
    o_ref[...] = m.astype(BF16)


def _merge(l, n_tiles, a_all, b_all, p, w_pa, w_pb, w_pc, conv_c):
    cw3 = 3 * C_WIDTH
    hb = TM // HALO
    last_hb = T_ALL // HALO - 1
    return pl.pallas_call(
        _merge_kernel,
        grid=(n_tiles, D // TN),
        in_specs=[
            pl.BlockSpec((TM, A_HEADS * HD), lambda i, j: (i, 0)),
            pl.BlockSpec((TM, B_HEADS * HD), lambda i, j: (i, 0)),
            pl.BlockSpec((TM, cw3), lambda i, j: (i, OFF_UC // cw3)),
            pl.BlockSpec((HALO, cw3), lambda i, j: (jnp.maximum(i * hb - 1, 0), OFF_UC // cw3)),
            pl.BlockSpec((HALO, cw3), lambda i, j: (jnp.minimum((i + 1) * hb, last_hb), OFF_UC // cw3)),
            pl.BlockSpec((TM, TN), lambda i, j: (i, OFF_ZA // TN + j)),
            pl.BlockSpec((TM, TN), lambda i, j: (i, OFF_ZB // TN + j)),
            pl.BlockSpec((TM, TN), lambda i, j: (i, OFF_ZC // TN + j)),
            pl.BlockSpec((None, A_HEADS * HD, TN), lambda i, j: (l, 0, j)),
            pl.BlockSpec((None, B_HEADS * HD, TN), lambda i, j: (l, 0, j)),
            pl.BlockSpec((None, C_WIDTH, TN), lambda i, j: (l, 0, j)),
            pl.BlockSpec((None, 3, C_WIDTH), lambda i, j: (l, 0, 0)),
        ],
        out_specs=pl.BlockSpec((TM, TN), lambda i, j: (i, j)),
        out_shape=jax.ShapeDtypeStruct((T_ALL, D), BF16),
        scratch_shapes=[pltpu.VMEM((TM, C_WIDTH), BF16)],
        compiler_params=_cparams("arbitrary", "arbitrary"),
        name="merge",
    )(a_all, b_all, p, p, p, p, p, p, w_pa, w_pb, w_pc, conv_c)


def _oproj_kernel(m_ref, w_ref, x_ref, gate_ref, o_ref):
    y = jnp.dot(m_ref[...], w_ref[...], preferred_element_type=F32)
    o_ref[...] = x_ref[...] + gate_ref[...] * y


def _oproj(l, n_tiles, m, w_o, xall, mods5):
    return pl.pallas_call(
        _oproj_kernel,
        grid=(n_tiles, D // TN),
        in_specs=[
            pl.BlockSpec((TM, D), lambda i, j: (i, 0)),
            pl.BlockSpec((None, D, TN), lambda i, j: (l, 0, j)),
            pl.BlockSpec((TM, TN), lambda i, j: (i, j)),
            _mod_spec(l, 2, TN),
        ],
        out_specs=pl.BlockSpec((TM, TN), lambda i, j: (i, j)),
        out_shape=jax.ShapeDtypeStruct(xall.shape, F32),
        input_output_aliases={2: 0},
        compiler_params=_cparams("arbitrary", "arbitrary"),
        name="oproj",
    )(m, w_o, xall, mods5)


def _ffn_kernel(x_ref, xp_ref, xn_ref, g_ref, shift_ref, scale_ref, gate_ref,
                wug_ref, wuv_ref, cg_ref, cv_ref, wd_ref, nf_ref, o_ref, h_ref, *, final_norm):
    i = pl.program_id(0)
    j = pl.program_id(1)
    nj = pl.num_programs(1)

    @pl.when(j == 0)
    def _():
        g, sh, sc = g_ref[...], shift_ref[...], scale_ref[...]
        h_ref[0:HALO, :] = _norm_mod(xp_ref[...], g, sh, sc).astype(BF16)
        h_ref[HALO:HALO + TM, :] = _norm_mod(x_ref[...], g, sh, sc).astype(BF16)
        h_ref[HALO + TM:, :] = _norm_mod(xn_ref[...], g, sh, sc).astype(BF16)

    h = h_ref[...]
    rows = TM + 2 * HALO
    has_prev, has_next = _seq_masks(i, (TM, TN))

    def conv(wu_ref, cw_ref):
        u = jnp.dot(h, wu_ref[...], preferred_element_type=F32)
        u_prev = pltpu.roll(u, 1, axis=0)[HALO:HALO + TM, :]
        u_next = pltpu.roll(u, rows - 1, axis=0)[HALO:HALO + TM, :]
        cw = cw_ref[...]
        return (cw[0:1, :] * jnp.where(has_prev, u_prev, 0.0) + cw[1:2, :] * u[HALO:HALO + TM, :]
                + cw[2:3, :] * jnp.where(has_next, u_next, 0.0))

    cgate = conv(wug_ref, cg_ref)
    cval = conv(wuv_ref, cv_ref)
    act = (cgate * jax.nn.sigmoid(cgate) * cval).astype(BF16)
    y = jnp.dot(act, wd_ref[...], preferred_element_type=F32)

    @pl.when(j == 0)
    def _():
        o_ref[...] = y

    @pl.when(j > 0)
    def _():
        o_ref[...] += y

    @pl.when(j == nj - 1)
    def _():
        xo = x_ref[...] + gate_ref[...] * o_ref[...]
        if final_norm:
            ms = jnp.mean(xo * xo, axis=-1, keepdims=True)
            xo = xo * lax.rsqrt(ms + EPS) * nf_ref[...]
        o_ref[...] = xo


def _ffn(l, n_tiles, xall, norm2, mods5, w_up, conv_f, w_down, norm_f, *, final_norm):
    hb = TM // HALO
    last_hb = T_ALL // HALO - 1
    nj = D_FF // TN
    out_rows = n_tiles * TM if final_norm else T_ALL
    kern = functools.partial(_ffn_kernel, final_norm=final_norm)
    return pl.pallas_call(
        kern,
        grid=(n_tiles, nj),
        in_specs=[
            pl.BlockSpec((TM, D), lambda i, j: (i, 0)),
            pl.BlockSpec((HALO, D), lambda i, j: (jnp.maximum(i * hb - 1, 0), 0)),
            pl.BlockSpec((HALO, D), lambda i, j: (jnp.minimum((i + 1) * hb, last_hb), 0)),
            pl.BlockSpec((None, 1, D), lambda i, j: (l, 0, 0)),
            _mod_spec(l, 3),
            _mod_spec(l, 4),
            _mod_spec(l, 5),
            pl.BlockSpec((None, D, TN), lambda i, j: (l, 0, j)),
            pl.BlockSpec((None, D, TN), lambda i, j: (l, 0, nj + j)),
            pl.BlockSpec((None, 3, TN), lambda i, j: (l, 0, j)),
            pl.BlockSpec((None, 3, TN), lambda i, j: (l, 0, nj + j)),
            pl.BlockSpec((None, TN, D), lambda i, j: (l, j, 0)),
            pl.BlockSpec((1, D), lambda i, j: (0, 0)),
        ],
        out_specs=pl.BlockSpec((TM, D), lambda i, j: (i, 0)),
        out_shape=jax.ShapeDtypeStruct((out_rows, D), F32),
        input_output_aliases={} if final_norm else {0: 0},
        scratch_shapes=[pltpu.VMEM((TM + 2 * HALO, D), BF16)],
        compiler_params=_cparams("arbitrary", "arbitrary"),
        name="ffn_final" if final_norm else "ffn",
    )(xall, xall, xall, norm2.reshape(DEPTH, 1, D), mods5, mods5, mods5,
      w_up, w_up, conv_f, conv_f, w_down, norm_f.reshape(1, D))


def _rope_tables():
    t = jnp.arange(S)
    row = (t // GRID_W).astype(F32)
    col = (t % GRID_W).astype(F32)
    n_freq = HD // 4
    inv = 1.0 / (ROPE_THETA ** (jnp.arange(n_freq, dtype=F32) / n_freq))
    ar = row[:, None] * inv
    ac = col[:, None] * inv
    ang = jnp.concatenate([ar, ar, ac, ac], axis=-1)
    lane = jnp.arange(HD)
    sign = jnp.where((lane & 32) == 0, -1.0, 1.0).astype(F32)
    return jnp.cos(ang), jnp.sin(ang) * sign


def _permute_in_cols(w):
    return jnp.concatenate([
        w[..., 0:1024],
        w[..., 1536:2560],
        w[..., 2560:3584],
        w[..., 4608:7680],
        w[..., 7680:13824],
        w[..., 3584:4608],
        w[..., 1024:1536],
    ], axis=-1)


def kernel(x, c, ctx, c_ctx, w_mod, b_mod, norm1, w_in, sink, rpb, conv_c, w_pa, w_pb, w_pc, w_o,
           norm2, w_up, conv_f, w_down, norm_f):
    c8 = jnp.concatenate([c, c_ctx[None, :], jnp.zeros((8 - NB - 1, D), F32)], axis=0)
    mods = _mods(c8, w_mod, b_mod)
    mods5 = mods.reshape(DEPTH, 8, N_MOD, 1, D)
    cos, sin_signed = _rope_tables()
    tables = _bias_tables(rpb)
    w_in_b = _permute_in_cols(w_in).astype(BF16)
    w_pa_b, w_pb_b, w_pc_b = w_pa.astype(BF16), w_pb.astype(BF16), w_pc.astype(BF16)
    w_o_b, w_up_b, w_down_b = w_o.astype(BF16), w_up.astype(BF16), w_down.astype(BF16)

    xall = jnp.concatenate([x.reshape(T_LAT, D), ctx.reshape(T_CTX, D)], axis=0)
    for l in range(DEPTH):
        last = l == DEPTH - 1
        n_tiles = LAT_TILES if last else ALL_TILES
        p = _inproj(l, xall, norm1, mods5, w_in_b, cos, sin_signed)
        a_all = _attn_a(p, sink[l])
        b_all = _attn_b(l, p, tables)
        if not last:
            a_all = _ctx_attn(p, sink[l], a_all, off_q=OFF_QA, off_k=OFF_KA, off_v=OFF_VA,
                              n_kv=A_KV, rep=A_REP, has_sink=True)
            b_all = _ctx_attn(p, sink[l], b_all, off_q=OFF_QB, off_k=OFF_KB, off_v=OFF_VB,
                              n_kv=B_HEADS, rep=1, has_sink=False)
        m = _merge(l, n_tiles, a_all, b_all, p, w_pa_b, w_pb_b, w_pc_b, conv_c)
        xall = _oproj(l, n_tiles, m, w_o_b, xall, mods5)
        xall = _ffn(l, n_tiles, xall, norm2, mods5, w_up_b, conv_f, w_down_b, norm_f, final_norm=last)
    return xall.reshape(NB, S, D)
```

```python
import functools

import numpy as np
import jax
import jax.numpy as jnp
from jax import lax
from jax.experimental import pallas as pl
from jax.experimental.pallas import tpu as pltpu

F32 = jnp.float32
BF16 = jnp.bfloat16

D = 2048
NB = 4
S = 4096
DEPTH = 4
GRID_W = 64
CTX = 256
HD = 128
A_HEADS = 8
A_KV = 2
A_REP = A_HEADS // A_KV
A_WIN = 128
A_BLOCK = 128
B_HEADS = 8
B_WIN_ROWS = 8
B_WIN_COLS = 16
C_WIDTH = 1024
D_FF = 5632
N_MOD = 6
ROPE_THETA = 10000.0
EPS = 1e-6
NEG = -1e30
SCALE = HD ** -0.5

T_LAT = NB * S
T_CTX = NB * CTX
T_ALL = T_LAT + T_CTX

OFF_QA = 0
OFF_QB = 1024
OFF_KB = 2048
OFF_UC = 3072
OFF_ZA = 6144
OFF_ZB = 8192
OFF_ZC = 10240
OFF_VB = 12288
OFF_KA = 13312
OFF_VA = 13568
IN_COLS = 13824

TM = 512
TN = 512
LAT_TILES = T_LAT // TM
ALL_TILES = T_ALL // TM
HALO = 8

VMEM_LIMIT = 56 * 1024 * 1024


def _cparams(*sem):
    return pltpu.CompilerParams(dimension_semantics=sem, vmem_limit_bytes=VMEM_LIMIT)


def _mod_row(i, tm):
    return jnp.minimum(i // (S // tm), NB)


def _mod_spec(l, k, tm=TM, ncols=None):
    if ncols is None:
        return pl.BlockSpec((None, None, None, 1, D), lambda i, j: (l, _mod_row(i, tm), k, 0, 0))
    return pl.BlockSpec((None, None, None, 1, ncols), lambda i, j: (l, _mod_row(i, tm), k, 0, j))


def _mods_kernel(c_ref, w_ref, b_ref, o_ref):
    cv = c_ref[...]
    sc = cv * jax.nn.sigmoid(cv)
    o_ref[...] = jnp.dot(sc, w_ref[...], preferred_element_type=F32,
                         precision=lax.Precision.HIGHEST) + b_ref[...]


def _mods(c8, w_mod, b_mod):
    tn = 1024
    ncols = N_MOD * D
    return pl.pallas_call(
        _mods_kernel,
        grid=(DEPTH, ncols // tn),
        in_specs=[
            pl.BlockSpec((8, D), lambda l, j: (0, 0)),
            pl.BlockSpec((None, D, tn), lambda l, j: (l, 0, j)),
            pl.BlockSpec((None, 1, tn), lambda l, j: (l, 0, j)),
        ],
        out_specs=pl.BlockSpec((None, 8, tn), lambda l, j: (l, 0, j)),
        out_shape=jax.ShapeDtypeStruct((DEPTH, 8, ncols), F32),
        compiler_params=_cparams("arbitrary", "arbitrary"),
        name="mods",
    )(c8, w_mod, b_mod.reshape(DEPTH, 1, ncols))


def _norm_mod(x, g, shift, scale):
    ms = jnp.mean(x * x, axis=-1, keepdims=True)
    y = x * lax.rsqrt(ms + EPS)
    y = y * g
    return y * (1.0 + scale) + shift


def _rope(u, cos, sin_signed, swap_lo):
    rot = jnp.where(swap_lo, pltpu.roll(u, 96, axis=1), pltpu.roll(u, 32, axis=1))
    return u * cos + rot * sin_signed


TM_IN = 1024
TN_IN = 1536
ROPE_CHUNKS = {
    OFF_QA // TN_IN: tuple(range((OFF_QB - OFF_QA) // HD)),
    OFF_KA // TN_IN: tuple((OFF_KA % TN_IN) // HD + c for c in range((OFF_VA - OFF_KA) // HD)),
}


def _inproj_kernel(x_ref, g_ref, shift_ref, scale_ref, w_ref, cos_ref, sin_ref, o_ref, h_ref):
    i = pl.program_id(0)
    j = pl.program_id(1)

    @pl.when(j == 0)
    def _():
        h = _norm_mod(x_ref[...], g_ref[...], shift_ref[...], scale_ref[...])
        h_ref[...] = h.astype(BF16)

    acc = jnp.dot(h_ref[...], w_ref[...], preferred_element_type=F32)
    o_ref[...] = acc.astype(BF16)

    for tile, chunks in ROPE_CHUNKS.items():
        @pl.when(jnp.logical_and(i < T_LAT // TM_IN, j == tile))
        def _():
            lane = lax.broadcasted_iota(jnp.int32, (TM_IN, HD), 1)
            swap_lo = (lane & 32) == 0
            for ch in chunks:
                cols = slice(ch * HD, (ch + 1) * HD)
                o_ref[:, cols] = _rope(acc[:, cols], cos_ref[...], sin_ref[...], swap_lo).astype(BF16)


def _inproj(l, xall, norm1, mods5, w_in, cos, sin_signed):
    tiles_per_seq = S // TM_IN
    return pl.pallas_call(
        _inproj_kernel,
        grid=(T_ALL // TM_IN, IN_COLS // TN_IN),
        in_specs=[
            pl.BlockSpec((TM_IN, D), lambda i, j: (i, 0)),
            pl.BlockSpec((None, 1, D), lambda i, j: (l, 0, 0)),
            _mod_spec(l, 0, TM_IN),
            _mod_spec(l, 1, TM_IN),
            pl.BlockSpec((None, D, TN_IN), lambda i, j: (l, 0, j)),
            pl.BlockSpec((TM_IN, HD), lambda i, j: (i % tiles_per_seq, 0)),
            pl.BlockSpec((TM_IN, HD), lambda i, j: (i % tiles_per_seq, 0)),
        ],
        out_specs=pl.BlockSpec((TM_IN, TN_IN), lambda i, j: (i, j)),
        out_shape=jax.ShapeDtypeStruct((T_ALL, IN_COLS), BF16),
        scratch_shapes=[pltpu.VMEM((TM_IN, D), BF16)],
        compiler_params=_cparams("arbitrary", "arbitrary"),
        name="inproj",
    )(xall, norm1.reshape(DEPTH, 1, D), mods5, mods5, w_in, cos, sin_signed)


A_KEYS = 3 * A_BLOCK


def _stack_heads(q, rep):
    if rep == 1:
        return q
    return jnp.concatenate([q[:, r * HD:(r + 1) * HD] for r in range(rep)], axis=0)


def _dot_nt(a, b):
    return lax.dot_general(a, b, (((1,), (1,)), ((), ())), preferred_element_type=F32)


def _attn_a_kernel(sink_ref, q_ref, k_ref, v_ref, kc_ref, vc_ref, o_ref):
    g = pl.program_id(1)
    n = pl.program_id(2)
    q4 = _stack_heads(q_ref[...], A_REP)
    start = pl.multiple_of(jnp.clip(n * A_BLOCK - A_BLOCK, 0, S - A_KEYS), A_BLOCK)
    k = k_ref[pl.ds(start, A_KEYS), :]
    v = v_ref[pl.ds(start, A_KEYS), :]
    rows = A_REP * A_BLOCK
    s_loc = _dot_nt(q4, k) * SCALE
    s_ctx = _dot_nt(q4, kc_ref[...]) * SCALE
    qpos = n * A_BLOCK + (lax.broadcasted_iota(jnp.int32, (rows, A_KEYS), 0) & (A_BLOCK - 1))
    kpos = start + lax.broadcasted_iota(jnp.int32, (rows, A_KEYS), 1)
    s_loc = jnp.where(jnp.abs(kpos - qpos) <= A_WIN, s_loc, NEG)
    ridx = lax.broadcasted_iota(jnp.int32, (rows, 1), 0)
    sink = jnp.full((rows, 1), sink_ref[g * A_REP], F32)
    for r in range(1, A_REP):
        sink = jnp.where(ridx >= r * A_BLOCK, sink_ref[g * A_REP + r], sink)
    m = jnp.maximum(jnp.maximum(jnp.max(s_loc, axis=1, keepdims=True),
                                jnp.max(s_ctx, axis=1, keepdims=True)), sink)
    e_loc = jnp.exp(s_loc - m)
    e_ctx = jnp.exp(s_ctx - m)
    denom = (jnp.sum(e_loc, axis=1, keepdims=True) + jnp.sum(e_ctx, axis=1, keepdims=True)
             + jnp.exp(sink - m))
    o = (jnp.dot(e_loc.astype(BF16), v, preferred_element_type=F32)
         + jnp.dot(e_ctx.astype(BF16), vc_ref[...], preferred_element_type=F32))
    o = o / denom
    for r in range(A_REP):
        o_ref[:, r * HD:(r + 1) * HD] = o[r * A_BLOCK:(r + 1) * A_BLOCK, :].astype(BF16)


def _attn_a(p, sink_l):
    nblk = S // A_BLOCK
    qw = A_REP * HD
    return pl.pallas_call(
        _attn_a_kernel,
        grid_spec=pltpu.PrefetchScalarGridSpec(
            num_scalar_prefetch=1,
            grid=(NB, A_KV, nblk),
            in_specs=[
                pl.BlockSpec((A_BLOCK, qw), lambda b, g, n, s: (b * nblk + n, g)),
                pl.BlockSpec((S, HD), lambda b, g, n, s: (b, OFF_KA // HD + g)),
                pl.BlockSpec((S, HD), lambda b, g, n, s: (b, OFF_VA // HD + g)),
                pl.BlockSpec((CTX, HD), lambda b, g, n, s: (T_LAT // CTX + b, OFF_KA // HD + g)),
                pl.BlockSpec((CTX, HD), lambda b, g, n, s: (T_LAT // CTX + b, OFF_VA // HD + g)),
            ],
            out_specs=pl.BlockSpec((A_BLOCK, qw), lambda b, g, n, s: (b * nblk + n, g)),
        ),
        out_shape=jax.ShapeDtypeStruct((T_ALL, A_HEADS * HD), BF16),
        compiler_params=_cparams("arbitrary", "arbitrary", "arbitrary"),
        name="attn_a",
    )(sink_l, p, p, p, p, p)


B_GROUP = 8
B_Q = B_GROUP * GRID_W
B_KROWS = 16
B_K = B_KROWS * GRID_W
B_NTAB = 2 * B_WIN_ROWS - 2


def _bias_tables(rpb):
    c = np.arange(GRID_W)[:, None]
    kc = np.arange(GRID_W)[None, :]
    cs = np.clip(c - B_WIN_COLS // 2, 0, GRID_W - B_WIN_COLS)
    ok = (kc >= cs) & (kc < cs + B_WIN_COLS)
    idx = np.clip(kc - c, 1 - B_WIN_COLS, B_WIN_COLS - 1) + B_WIN_COLS - 1
    mt = jnp.where(jnp.asarray(ok), rpb[..., idx].astype(F32), NEG)
    return jnp.concatenate([mt[:, :, :-1], mt[:, :, 1:]], axis=-1)


def _attn_b_kernel(q_ref, k_ref, v_ref, kc_ref, vc_ref, t_ref, o_ref, s_ref, p_ref, pc_ref, inv_ref):
    g = pl.program_id(2)
    n_groups = (S // GRID_W) // B_GROUP
    r0 = g * B_GROUP
    base = jnp.clip(r0 - B_WIN_ROWS // 2, 0, S // GRID_W - B_KROWS)
    kstart = pl.multiple_of(base * GRID_W, 4 * GRID_W)
    q = q_ref[...]
    k = k_ref[pl.ds(kstart, B_K), :]
    v = v_ref[pl.ds(kstart, B_K), :]
    s_ref[...] = _dot_nt(q, k)
    sc_all = _dot_nt(q, kc_ref[...]) * SCALE
    p_ref[...] = jnp.zeros_like(p_ref)
    lane = lax.broadcasted_iota(jnp.int32, (GRID_W, 2 * GRID_W), 1)
    left = lane < GRID_W

    def body(off, jlo_of):
        for i in range(B_GROUP):
            jlo = jlo_of(i)
            p0, p1 = jlo // 2, (jlo + B_WIN_ROWS - 1) // 2
            rows = slice(i * GRID_W, (i + 1) * GRID_W)
            cols = slice(2 * GRID_W * p0, 2 * GRID_W * (p1 + 1))
            tiles = []
            for pp in range(p0, p1 + 1):
                t = t_ref[off + 2 * pp - i + B_WIN_ROWS - 1]
                if 2 * pp < jlo:
                    t = jnp.where(left, NEG, t)
                if 2 * pp + 1 >= jlo + B_WIN_ROWS:
                    t = jnp.where(left, t, NEG)
                tiles.append(t)
            sl = s_ref[rows, cols] * SCALE + jnp.concatenate(tiles, axis=1)
            sc = sc_all[rows, :]
            m = jnp.maximum(jnp.max(sl, axis=1, keepdims=True), jnp.max(sc, axis=1, keepdims=True))
            e = jnp.exp(sl - m)
            ec = jnp.exp(sc - m)
            denom = jnp.sum(e, axis=1, keepdims=True) + jnp.sum(ec, axis=1, keepdims=True)
            p_ref[rows, cols] = e.astype(BF16)
            pc_ref[rows, :] = ec.astype(BF16)
            inv_ref[rows, :] = 1.0 / denom

    half = B_WIN_ROWS // 2

    @pl.when(g == 0)
    def _():
        body(0, lambda i: max(i - half, 0))

    @pl.when(jnp.logical_and(g > 0, g < n_groups - 1))
    def _():
        body(-half, lambda i: i)

    @pl.when(g == n_groups - 1)
    def _():
        body(-(B_KROWS - B_GROUP), lambda i: min(i + half, B_KROWS - B_WIN_ROWS))

    o = (jnp.dot(p_ref[...], v, preferred_element_type=F32)
         + jnp.dot(pc_ref[...], vc_ref[...], preferred_element_type=F32))
    o_ref[...] = (o * inv_ref[...]).astype(BF16)


def _attn_b(l, p, tables):
    n_groups = (S // GRID_W) // B_GROUP
    return pl.pallas_call(
        _attn_b_kernel,
        grid=(B_HEADS, NB, n_groups),
        in_specs=[
            pl.BlockSpec((B_Q, HD), lambda h, b, g: (b * n_groups + g, OFF_QB // HD + h)),
            pl.BlockSpec((S, HD), lambda h, b, g: (b, OFF_KB // HD + h)),
            pl.BlockSpec((S, HD), lambda h, b, g: (b, OFF_VB // HD + h)),
            pl.BlockSpec((CTX, HD), lambda h, b, g: (T_LAT // CTX + b, OFF_KB // HD + h)),
            pl.BlockSpec((CTX, HD), lambda h, b, g: (T_LAT // CTX + b, OFF_VB // HD + h)),
            pl.BlockSpec((None, None, B_NTAB, GRID_W, 2 * GRID_W), lambda h, b, g: (l, h, 0, 0, 0)),
        ],
        out_specs=pl.BlockSpec((B_Q, HD), lambda h, b, g: (b * n_groups + g, h)),
        out_shape=jax.ShapeDtypeStruct((T_ALL, B_HEADS * HD), BF16),
        scratch_shapes=[
            pltpu.VMEM((B_Q, B_K), F32),
            pltpu.VMEM((B_Q, B_K), BF16),
            pltpu.VMEM((B_Q, CTX), BF16),
            pltpu.VMEM((B_Q, 1), F32),
        ],
        compiler_params=_cparams("arbitrary", "arbitrary", "arbitrary"),
        name="attn_b",
    )(p, p, p, p, p, tables)


def _ctx_attn_kernel(sink_ref, q_ref, k_ref, v_ref, prev_ref, o_ref, *, rep, has_sink):
    del prev_ref
    g = pl.program_id(1)
    qs = _stack_heads(q_ref[...], rep)
    s = _dot_nt(qs, k_ref[...]) * SCALE
    m = jnp.max(s, axis=1, keepdims=True)
    rows = rep * CTX
    if has_sink:
        ridx = lax.broadcasted_iota(jnp.int32, (rows, 1), 0)
        sink = jnp.full((rows, 1), sink_ref[g * rep], F32)
        for r in range(1, rep):
            sink = jnp.where(ridx >= r * CTX, sink_ref[g * rep + r], sink)
        m = jnp.maximum(m, sink)
    e = jnp.exp(s - m)
    denom = jnp.sum(e, axis=1, keepdims=True)
    if has_sink:
        denom = denom + jnp.exp(sink - m)
    o = jnp.dot(e.astype(BF16), v_ref[...], preferred_element_type=F32) / denom
    for r in range(rep):
        o_ref[:, r * HD:(r + 1) * HD] = o[r * CTX:(r + 1) * CTX, :].astype(BF16)


def _ctx_attn(p, sink_l, prev, *, off_q, off_k, off_v, n_kv, rep, has_sink):
    qw = rep * HD
    rb = T_LAT // CTX
    kern = functools.partial(_ctx_attn_kernel, rep=rep, has_sink=has_sink)
    return pl.pallas_call(
        kern,
        grid_spec=pltpu.PrefetchScalarGridSpec(
            num_scalar_prefetch=1,
            grid=(NB, n_kv),
            in_specs=[
                pl.BlockSpec((CTX, qw), lambda b, g, s: (rb + b, off_q // qw + g)),
                pl.BlockSpec((CTX, HD), lambda b, g, s: (rb + b, off_k // HD + g)),
                pl.BlockSpec((CTX, HD), lambda b, g, s: (rb + b, off_v // HD + g)),
                pl.BlockSpec(memory_space=pl.ANY),
            ],
            out_specs=pl.BlockSpec((CTX, qw), lambda b, g, s: (rb + b, g)),
        ),
        out_shape=jax.ShapeDtypeStruct(prev.shape, prev.dtype),
        input_output_aliases={4: 0},
        compiler_params=_cparams("arbitrary", "arbitrary"),
        name="ctx_attn_a" if has_sink else "ctx_attn_b",
    )(sink_l, p, p, p, prev)


def _seq_masks(i, shape):
    lmask = jnp.where(i < LAT_TILES, S - 1, CTX - 1)
    pos = (i * TM + lax.broadcasted_iota(jnp.int32, shape, 0)) & lmask
    return pos != 0, pos != lmask


def _merge_kernel(a_ref, b_ref, c_ref, cp_ref, cn_ref, za_ref, zb_ref, zc_ref,
                  wa_ref, wb_ref, wc_ref, cw_ref, o_ref, cb_ref):
    i = pl.program_id(0)
    j = pl.program_id(1)

    @pl.when(j == 0)
    def _():
        w = C_WIDTH

        def gated(ref):
            return ref[:, w:2 * w].astype(F32) * ref[:, 0:w].astype(F32)

        v = gated(c_ref)
        vp = gated(cp_ref)[HALO - 1:HALO, :]
        vn = gated(cn_ref)[0:1, :]
        row = lax.broadcasted_iota(jnp.int32, (TM, w), 0)
        v_prev = jnp.where(row == 0, vp, pltpu.roll(v, 1, axis=0))
        v_next = jnp.where(row == TM - 1, vn, pltpu.roll(v, TM - 1, axis=0))
        has_prev, has_next = _seq_masks(i, (TM, w))
        cw = cw_ref[...]
        conv = (cw[0:1, :] * jnp.where(has_prev, v_prev, 0.0) + cw[1:2, :] * v
                + cw[2:3, :] * jnp.where(has_next, v_next, 0.0))
        cb_ref[...] = (c_ref[:, 2 * w:3 * w].astype(F32) * conv).astype(BF16)

    ya = jnp.dot(a_ref[...], wa_ref[...], preferred_element_type=F32)
    yb = jnp.dot(b_ref[...], wb_ref[...], preferred_element_type=F32)
    yc = jnp.dot(cb_ref[...], wc_ref[...], preferred_element_type=F32)
    m = (jax.nn.sigmoid(za_ref[...].astype(F32)) * ya + jax.nn.sigmoid(zb_ref[...].astype(F32)) * yb
         + jax.nn.sigmoid(zc_ref[...].astype(F32)) * yc)
    o_ref[...] = m.astype(BF16)


def _merge(l, n_tiles, a_all, b_all, p, w_pa, w_pb, w_pc, conv_c):
    cw3 = 3 * C_WIDTH
    hb = TM // HALO
    last_hb = T_ALL // HALO - 1
    return pl.pallas_call(
        _merge_kernel,
        grid=(n_tiles, D // TN),
        in_specs=[
            pl.BlockSpec((TM, A_HEADS * HD), lambda i, j: (i, 0)),
            pl.BlockSpec((TM, B_HEADS * HD), lambda i, j: (i, 0)),
            pl.BlockSpec((TM, cw3), lambda i, j: (i, OFF_UC // cw3)),
            pl.BlockSpec((HALO, cw3), lambda i, j: (jnp.maximum(i * hb - 1, 0), OFF_UC // cw3)),
            pl.BlockSpec((HALO, cw3), lambda i, j: (jnp.minimum((i + 1) * hb, last_hb), OFF_UC // cw3)),
            pl.BlockSpec((TM, TN), lambda i, j: (i, OFF_ZA // TN + j)),
            pl.BlockSpec((TM, TN), lambda i, j: (i, OFF_ZB // TN + j)),
            pl.BlockSpec((TM, TN), lambda i, j: (i, OFF_ZC // TN + j)),
            pl.BlockSpec((None, A_HEADS * HD, TN), lambda i, j: (l, 0, j)),
            pl.BlockSpec((None, B_HEADS * HD, TN), lambda i, j: (l, 0, j)),
            pl.BlockSpec((None, C_WIDTH, TN), lambda i, j: (l, 0, j)),
            pl.BlockSpec((None, 3, C_WIDTH), lambda i, j: (l, 0, 0)),
        ],
        out_specs=pl.BlockSpec((TM, TN), lambda i, j: (i, j)),
        out_shape=jax.ShapeDtypeStruct((T_ALL, D), BF16),
        scratch_shapes=[pltpu.VMEM((TM, C_WIDTH), BF16)],
        compiler_params=_cparams("arbitrary", "arbitrary"),
        name="merge",
    )(a_all, b_all, p, p, p, p, p, p, w_pa, w_pb, w_pc, conv_c)


def _oproj_kernel(m_ref, w_ref, x_ref, gate_ref, o_ref):
    y = jnp.dot(m_ref[...], w_ref[...], preferred_element_type=F32)
    o_ref[...] = x_ref[...] + gate_ref[...] * y


def _oproj(l, n_tiles, m, w_o, xall, mods5):
    return pl.pallas_call(
        _oproj_kernel,
        grid=(n_tiles, D // TN),
        in_specs=[
            pl.BlockSpec((TM, D), lambda i, j: (i, 0)),
            pl.BlockSpec((None, D, TN), lambda i, j: (l, 0, j)),
            pl.BlockSpec((TM, TN), lambda i, j: (i, j)),
            _mod_spec(l, 2, TM, TN),
        ],
        out_specs=pl.BlockSpec((TM, TN), lambda i, j: (i, j)),
        out_shape=jax.ShapeDtypeStruct(xall.shape, F32),
        input_output_aliases={2: 0},
        compiler_params=_cparams("arbitrary", "arbitrary"),
        name="oproj",
    )(m, w_o, xall, mods5)


def _ffn_kernel(x_ref, xp_ref, xn_ref, g_ref, shift_ref, scale_ref, gate_ref,
                wug_ref, wuv_ref, cg_ref, cv_ref, wd_ref, nf_ref, o_ref, h_ref, act0_ref, act1_ref, ug0_ref, ug1_ref, uv0_ref, uv1_ref,
                *, final_norm, nj):
    i = pl.program_id(0)
    j = pl.program_id(1)
    n_slab = TN // HD
    last = nj + 1

    @pl.when(j == 0)
    def _():
        g, sh, sc = g_ref[...], shift_ref[...], scale_ref[...]
        h_ref[0:HALO, :] = _norm_mod(xp_ref[...], g, sh, sc).astype(BF16)
        h_ref[HALO:HALO + TM, :] = _norm_mod(x_ref[...], g, sh, sc).astype(BF16)
        h_ref[HALO + TM:, :] = _norm_mod(xn_ref[...], g, sh, sc).astype(BF16)
        o_ref[...] = jnp.zeros((TM, D), F32)

    act_ref = (act0_ref, act1_ref)
    ug_ref = (ug0_ref, ug1_ref)
    uv_ref = (uv0_ref, uv1_ref)

    def up(slot):
        h = h_ref[...]
        ug = jnp.dot(h, wug_ref[...], preferred_element_type=F32)
        for c in range(n_slab):
            ug_ref[slot][c] = ug[:, c * HD:(c + 1) * HD]
        uv = jnp.dot(h, wuv_ref[...], preferred_element_type=F32)
        for c in range(n_slab):
            uv_ref[slot][c] = uv[:, c * HD:(c + 1) * HD]

    def epilogue(slot):
        rb = 64
        lmask = jnp.where(i < LAT_TILES, S - 1, CTX - 1)

        def conv(u_ref, cw_ref, c, r0, has_prev, has_next):
            u_prev = jnp.where(has_prev, u_ref[slot][c, HALO - 1 + r0:HALO - 1 + r0 + rb, :], 0.0)
            u_next = jnp.where(has_next, u_ref[slot][c, HALO + 1 + r0:HALO + 1 + r0 + rb, :], 0.0)
            cw = cw_ref[:, c * HD:(c + 1) * HD]
            return (cw[0:1, :] * u_prev + cw[1:2, :] * u_ref[slot][c, HALO + r0:HALO + r0 + rb, :]
                    + cw[2:3, :] * u_next)

        for r0 in range(0, TM, rb):
            pos = (i * TM + r0 + lax.broadcasted_iota(jnp.int32, (rb, HD), 0)) & lmask
            has_prev, has_next = pos != 0, pos != lmask
            for c in range(n_slab):
                cgate = conv(ug_ref, cg_ref, c, r0, has_prev, has_next)
                cval = conv(uv_ref, cv_ref, c, r0, has_prev, has_next)
                act_ref[slot][r0:r0 + rb, c * HD:(c + 1) * HD] = (
                    cgate * jax.nn.sigmoid(cgate) * cval).astype(BF16)

    def down(slot):
        return jnp.dot(act_ref[slot][...], wd_ref[...], preferred_element_type=F32)

    @pl.when(j == 0)
    def _():
        up(0)

    @pl.when(j == 1)
    def _():
        up(1)
        epilogue(0)

    for parity in (0, 1):
        @pl.when(jnp.logical_and(jnp.logical_and(j >= 2, j < nj), j % 2 == parity))
        def _():
            epilogue(1 - parity)
            up(parity)
            o_ref[...] += down(parity)

    @pl.when(j == nj)
    def _():
        o_ref[...] += down(nj % 2)
        epilogue((nj - 1) % 2)

    @pl.when(j == last)
    def _():
        xo = x_ref[...] + gate_ref[...] * (o_ref[...] + down((nj - 1) % 2))
        if final_norm:
            ms = jnp.mean(xo * xo, axis=-1, keepdims=True)
            xo = xo * lax.rsqrt(ms + EPS) * nf_ref[...]
        o_ref[...] = xo


def _ffn(l, n_tiles, xall, norm2, mods5, w_up, conv_f, w_down, norm_f, *, final_norm):
    hb = TM // HALO
    last_hb = T_ALL // HALO - 1
    nj = D_FF // TN
    out_rows = n_tiles * TM if final_norm else T_ALL
    kern = functools.partial(_ffn_kernel, final_norm=final_norm, nj=nj)

    def chunk(j, lag):
        return jnp.clip(j - lag, 0, nj - 1)

    return pl.pallas_call(
        kern,
        grid=(n_tiles, nj + 2),
        in_specs=[
            pl.BlockSpec((TM, D), lambda i, j: (i, 0)),
            pl.BlockSpec((HALO, D), lambda i, j: (jnp.maximum(i * hb - 1, 0), 0)),
            pl.BlockSpec((HALO, D), lambda i, j: (jnp.minimum((i + 1) * hb, last_hb), 0)),
            pl.BlockSpec((None, 1, D), lambda i, j: (l, 0, 0)),
            _mod_spec(l, 3),
            _mod_spec(l, 4),
            _mod_spec(l, 5),
            pl.BlockSpec((None, D, TN), lambda i, j: (l, 0, chunk(j, 0))),
            pl.BlockSpec((None, D, TN), lambda i, j: (l, 0, nj + chunk(j, 0))),
            pl.BlockSpec((None, 3, TN), lambda i, j: (l, 0, chunk(j, 1))),
            pl.BlockSpec((None, 3, TN), lambda i, j: (l, 0, nj + chunk(j, 1))),
            pl.BlockSpec((None, TN, D), lambda i, j: (l, chunk(j, 2), 0)),
            pl.BlockSpec((1, D), lambda i, j: (0, 0)),
        ],
        out_specs=pl.BlockSpec((TM, D), lambda i, j: (i, 0)),
        out_shape=jax.ShapeDtypeStruct((out_rows, D), F32),
        scratch_shapes=([pltpu.VMEM((TM + 2 * HALO, D), BF16)] + 2 * [pltpu.VMEM((TM, TN), BF16)]
                        + 4 * [pltpu.VMEM((TN // HD, TM + 2 * HALO, HD), F32)]),
        compiler_params=_cparams("arbitrary", "arbitrary"),
        name="ffn_final" if final_norm else "ffn",
    )(xall, xall, xall, norm2.reshape(DEPTH, 1, D), mods5, mods5, mods5,
      w_up, w_up, conv_f, conv_f, w_down, norm_f.reshape(1, D))


def _rope_tables():
    t = jnp.arange(S)
    row = (t // GRID_W).astype(F32)
    col = (t % GRID_W).astype(F32)
    n_freq = HD // 4
    inv = 1.0 / (ROPE_THETA ** (jnp.arange(n_freq, dtype=F32) / n_freq))
    ar = row[:, None] * inv
    ac = col[:, None] * inv
    ang = jnp.concatenate([ar, ar, ac, ac], axis=-1)
    lane = jnp.arange(HD)
    sign = jnp.where((lane & 32) == 0, -1.0, 1.0).astype(F32)
    return jnp.cos(ang), jnp.sin(ang) * sign


def _permute_in_cols(w):
    return jnp.concatenate([
        w[..., 0:1024],
        w[..., 1536:2560],
        w[..., 2560:3584],
        w[..., 4608:7680],
        w[..., 7680:13824],
        w[..., 3584:4608],
        w[..., 1024:1536],
    ], axis=-1)


def kernel(x, c, ctx, c_ctx, w_mod, b_mod, norm1, w_in, sink, rpb, conv_c, w_pa, w_pb, w_pc, w_o,
           norm2, w_up, conv_f, w_down, norm_f):
    c8 = jnp.concatenate([c, c_ctx[None, :], jnp.zeros((8 - NB - 1, D), F32)], axis=0)
    mods = _mods(c8, w_mod, b_mod)
    mods5 = mods.reshape(DEPTH, 8, N_MOD, 1, D)
    cos, sin_signed = _rope_tables()
    tables = _bias_tables(rpb)
    w_in_b = _permute_in_cols(w_in).astype(BF16)
    w_pa_b, w_pb_b, w_pc_b = w_pa.astype(BF16), w_pb.astype(BF16), w_pc.astype(BF16)
    w_o_b, w_up_b, w_down_b = w_o.astype(BF16), w_up.astype(BF16), w_down.astype(BF16)

    xall = jnp.concatenate([x.reshape(T_LAT, D), ctx.reshape(T_CTX, D)], axis=0)
    for l in range(DEPTH):
        last = l == DEPTH - 1
        n_tiles = LAT_TILES if last else ALL_TILES
        p = _inproj(l, xall, norm1, mods5, w_in_b, cos, sin_signed)
        a_all = _attn_a(p, sink[l])
        b_all = _attn_b(l, p, tables)
        if not last:
            a_all = _ctx_attn(p, sink[l], a_all, off_q=OFF_QA, off_k=OFF_KA, off_v=OFF_VA,
                              n_kv=A_KV, rep=A_REP, has_sink=True)
            b_all = _ctx_attn(p, sink[l], b_all, off_q=OFF_QB, off_k=OFF_KB, off_v=OFF_VB,
                              n_kv=B_HEADS, rep=1, has_sink=False)
        m = _merge(l, n_tiles, a_all, b_all, p, w_pa_b, w_pb_b, w_pc_b, conv_c)
        xall = _oproj(l, n_tiles, m, w_o_b, xall, mods5)
        xall = _ffn(l, n_tiles, xall, norm2, mods5, w_up_b, conv_f, w_down_b, norm_f, final_norm=last)
    return xall.reshape(NB, S, D)
```

```python
import functools

import numpy as np
import jax
import jax.numpy as jnp
from jax import lax
from jax.experimental import pallas as pl
from jax.experimental.pallas import tpu as pltpu

F32 = jnp.float32
BF16 = jnp.bfloat16

D = 2048
NB = 4
S = 4096
DEPTH = 4
GRID_W = 64
CTX = 256
HD = 128
A_HEADS = 8
A_KV = 2
A_REP = A_HEADS // A_KV
A_WIN = 128
A_BLOCK = 128
B_HEADS = 8
B_WIN_ROWS = 8
B_WIN_COLS = 16
C_WIDTH = 1024
D_FF = 5632
N_MOD = 6
ROPE_THETA = 10000.0
EPS = 1e-6
NEG = -1e30
SCALE = HD ** -0.5

T_LAT = NB * S
T_CTX = NB * CTX
T_ALL = T_LAT + T_CTX

OFF_QA = 0
OFF_QB = 1024
OFF_KB = 2048
OFF_UC = 3072
OFF_ZA = 6144
OFF_ZB = 8192
OFF_ZC = 10240
OFF_VB = 12288
OFF_KA = 13312
OFF_VA = 13568
IN_COLS = 13824

HALO = 8

VMEM_LIMIT = 56 * 1024 * 1024


def _cparams(*sem):
    return pltpu.CompilerParams(dimension_semantics=sem, vmem_limit_bytes=VMEM_LIMIT)


def _mod_row(i, tm):
    return jnp.minimum(i // (S // tm), NB)


def _mod_spec(l, k, tm):
    return pl.BlockSpec((None, None, None, 1, D), lambda i, j: (l, _mod_row(i, tm), k, 0, 0))


def _mods_kernel(c_ref, w_ref, b_ref, o_ref):
    cv = c_ref[...]
    sc = cv * jax.nn.sigmoid(cv)
    o_ref[...] = jnp.dot(sc, w_ref[...], preferred_element_type=F32,
                         precision=lax.Precision.HIGHEST) + b_ref[...]


def _mods(c8, w_mod, b_mod):
    tn = 1024
    ncols = N_MOD * D
    return pl.pallas_call(
        _mods_kernel,
        grid=(DEPTH, ncols // tn),
        in_specs=[
            pl.BlockSpec((8, D), lambda l, j: (0, 0)),
            pl.BlockSpec((None, D, tn), lambda l, j: (l, 0, j)),
            pl.BlockSpec((None, 1, tn), lambda l, j: (l, 0, j)),
        ],
        out_specs=pl.BlockSpec((None, 8, tn), lambda l, j: (l, 0, j)),
        out_shape=jax.ShapeDtypeStruct((DEPTH, 8, ncols), F32),
        compiler_params=_cparams("arbitrary", "arbitrary"),
        name="mods",
    )(c8, w_mod, b_mod.reshape(DEPTH, 1, ncols))


def _norm_mod(x, g, shift, scale):
    ms = jnp.mean(x * x, axis=-1, keepdims=True)
    y = x * lax.rsqrt(ms + EPS)
    y = y * g
    return y * (1.0 + scale) + shift


def _rope(u, cos, sin_signed, swap_lo):
    rot = jnp.where(swap_lo, pltpu.roll(u, 96, axis=1), pltpu.roll(u, 32, axis=1))
    return u * cos + rot * sin_signed


TM_IN = 1024
TN_IN = 1536
ROPE_CHUNKS = {
    OFF_QA // TN_IN: tuple(range((OFF_QB - OFF_QA) // HD)),
    OFF_KA // TN_IN: tuple((OFF_KA % TN_IN) // HD + c for c in range((OFF_VA - OFF_KA) // HD)),
}


def _inproj_kernel(x_ref, g_ref, shift_ref, scale_ref, w_ref, cos_ref, sin_ref, o_ref, h_ref):
    i = pl.program_id(0)
    j = pl.program_id(1)

    @pl.when(j == 0)
    def _():
        h = _norm_mod(x_ref[...], g_ref[...], shift_ref[...], scale_ref[...])
        h_ref[...] = h.astype(BF16)

    acc = jnp.dot(h_ref[...], w_ref[...], preferred_element_type=F32)
    o_ref[...] = acc.astype(BF16)

    for tile, chunks in ROPE_CHUNKS.items():
        @pl.when(jnp.logical_and(i < T_LAT // TM_IN, j == tile))
        def _():
            lane = lax.broadcasted_iota(jnp.int32, (TM_IN, HD), 1)
            swap_lo = (lane & 32) == 0
            for ch in chunks:
                cols = slice(ch * HD, (ch + 1) * HD)
                o_ref[:, cols] = _rope(acc[:, cols], cos_ref[...], sin_ref[...], swap_lo).astype(BF16)


def _inproj(l, xall, norm1, mods5, w_in, cos, sin_signed):
    tiles_per_seq = S // TM_IN
    return pl.pallas_call(
        _inproj_kernel,
        grid=(T_ALL // TM_IN, IN_COLS // TN_IN),
        in_specs=[
            pl.BlockSpec((TM_IN, D), lambda i, j: (i, 0)),
            pl.BlockSpec((None, 1, D), lambda i, j: (l, 0, 0)),
            _mod_spec(l, 0, TM_IN),
            _mod_spec(l, 1, TM_IN),
            pl.BlockSpec((None, D, TN_IN), lambda i, j: (l, 0, j)),
            pl.BlockSpec((TM_IN, HD), lambda i, j: (i % tiles_per_seq, 0)),
            pl.BlockSpec((TM_IN, HD), lambda i, j: (i % tiles_per_seq, 0)),
        ],
        out_specs=pl.BlockSpec((TM_IN, TN_IN), lambda i, j: (i, j)),
        out_shape=jax.ShapeDtypeStruct((T_ALL, IN_COLS), BF16),
        scratch_shapes=[pltpu.VMEM((TM_IN, D), BF16)],
        compiler_params=_cparams("arbitrary", "arbitrary"),
        name="inproj",
    )(xall, norm1.reshape(DEPTH, 1, D), mods5, mods5, w_in, cos, sin_signed)


A_KEYS = 3 * A_BLOCK


def _stack_heads(q, rep):
    if rep == 1:
        return q
    return jnp.concatenate([q[:, r * HD:(r + 1) * HD] for r in range(rep)], axis=0)


def _dot_nt(a, b):
    return lax.dot_general(a, b, (((1,), (1,)), ((), ())), preferred_element_type=F32)


def _attn_a_kernel(sink_ref, q_ref, k_ref, v_ref, kc_ref, vc_ref, o_ref):
    g = pl.program_id(1)
    n = pl.program_id(2)
    q4 = _stack_heads(q_ref[...], A_REP)
    start = pl.multiple_of(jnp.clip(n * A_BLOCK - A_BLOCK, 0, S - A_KEYS), A_BLOCK)
    k = k_ref[pl.ds(start, A_KEYS), :]
    v = v_ref[pl.ds(start, A_KEYS), :]
    rows = A_REP * A_BLOCK
    s_loc = _dot_nt(q4, k) * SCALE
    s_ctx = _dot_nt(q4, kc_ref[...]) * SCALE
    qpos = n * A_BLOCK + (lax.broadcasted_iota(jnp.int32, (rows, A_KEYS), 0) & (A_BLOCK - 1))
    kpos = start + lax.broadcasted_iota(jnp.int32, (rows, A_KEYS), 1)
    s_loc = jnp.where(jnp.abs(kpos - qpos) <= A_WIN, s_loc, NEG)
    ridx = lax.broadcasted_iota(jnp.int32, (rows, 1), 0)
    sink = jnp.full((rows, 1), sink_ref[g * A_REP], F32)
    for r in range(1, A_REP):
        sink = jnp.where(ridx >= r * A_BLOCK, sink_ref[g * A_REP + r], sink)
    m = jnp.maximum(jnp.maximum(jnp.max(s_loc, axis=1, keepdims=True),
                                jnp.max(s_ctx, axis=1, keepdims=True)), sink)
    e_loc = jnp.exp(s_loc - m)
    e_ctx = jnp.exp(s_ctx - m)
    denom = (jnp.sum(e_loc, axis=1, keepdims=True) + jnp.sum(e_ctx, axis=1, keepdims=True)
             + jnp.exp(sink - m))
    o = (jnp.dot(e_loc.astype(BF16), v, preferred_element_type=F32)
         + jnp.dot(e_ctx.astype(BF16), vc_ref[...], preferred_element_type=F32))
    o = o / denom
    for r in range(A_REP):
        o_ref[:, r * HD:(r + 1) * HD] = o[r * A_BLOCK:(r + 1) * A_BLOCK, :].astype(BF16)


def _attn_a(p, sink_l):
    nblk = S // A_BLOCK
    qw = A_REP * HD
    return pl.pallas_call(
        _attn_a_kernel,
        grid_spec=pltpu.PrefetchScalarGridSpec(
            num_scalar_prefetch=1,
            grid=(NB, A_KV, nblk),
            in_specs=[
                pl.BlockSpec((A_BLOCK, qw), lambda b, g, n, s: (b * nblk + n, g)),
                pl.BlockSpec((S, HD), lambda b, g, n, s: (b, OFF_KA // HD + g)),
                pl.BlockSpec((S, HD), lambda b, g, n, s: (b, OFF_VA // HD + g)),
                pl.BlockSpec((CTX, HD), lambda b, g, n, s: (T_LAT // CTX + b, OFF_KA // HD + g)),
                pl.BlockSpec((CTX, HD), lambda b, g, n, s: (T_LAT // CTX + b, OFF_VA // HD + g)),
            ],
            out_specs=pl.BlockSpec((A_BLOCK, qw), lambda b, g, n, s: (b * nblk + n, g)),
        ),
        out_shape=jax.ShapeDtypeStruct((T_LAT, A_HEADS * HD), BF16),
        compiler_params=_cparams("arbitrary", "arbitrary", "arbitrary"),
        name="attn_a",
    )(sink_l, p, p, p, p, p)


B_GROUP = 8
B_Q = B_GROUP * GRID_W
B_KROWS = 16
B_K = B_KROWS * GRID_W
B_NTAB = 2 * B_WIN_ROWS - 2


def _bias_tables(rpb):
    c = np.arange(GRID_W)[:, None]
    kc = np.arange(GRID_W)[None, :]
    cs = np.clip(c - B_WIN_COLS // 2, 0, GRID_W - B_WIN_COLS)
    ok = (kc >= cs) & (kc < cs + B_WIN_COLS)
    idx = np.clip(kc - c, 1 - B_WIN_COLS, B_WIN_COLS - 1) + B_WIN_COLS - 1
    mt = jnp.where(jnp.asarray(ok), rpb[..., idx].astype(F32), NEG)
    return jnp.concatenate([mt[:, :, :-1], mt[:, :, 1:]], axis=-1)


def _attn_b_kernel(q_ref, k_ref, v_ref, kc_ref, vc_ref, t_ref, o_ref, s_ref, p_ref, pc_ref, inv_ref):
    g = pl.program_id(2)
    n_groups = (S // GRID_W) // B_GROUP
    r0 = g * B_GROUP
    base = jnp.clip(r0 - B_WIN_ROWS // 2, 0, S // GRID_W - B_KROWS)
    kstart = pl.multiple_of(base * GRID_W, 4 * GRID_W)
    q = q_ref[...]
    k = k_ref[pl.ds(kstart, B_K), :]
    v = v_ref[pl.ds(kstart, B_K), :]
    s_ref[...] = _dot_nt(q, k)
    sc_all = _dot_nt(q, kc_ref[...]) * SCALE
    p_ref[...] = jnp.zeros_like(p_ref)
    lane = lax.broadcasted_iota(jnp.int32, (GRID_W, 2 * GRID_W), 1)
    left = lane < GRID_W

    def body(off, jlo_of):
        for i in range(B_GROUP):
            jlo = jlo_of(i)
            p0, p1 = jlo // 2, (jlo + B_WIN_ROWS - 1) // 2
            rows = slice(i * GRID_W, (i + 1) * GRID_W)
            cols = slice(2 * GRID_W * p0, 2 * GRID_W * (p1 + 1))
            tiles = []
            for pp in range(p0, p1 + 1):
                t = t_ref[off + 2 * pp - i + B_WIN_ROWS - 1]
                if 2 * pp < jlo:
                    t = jnp.where(left, NEG, t)
                if 2 * pp + 1 >= jlo + B_WIN_ROWS:
                    t = jnp.where(left, t, NEG)
                tiles.append(t)
            sl = s_ref[rows, cols] * SCALE + jnp.concatenate(tiles, axis=1)
            sc = sc_all[rows, :]
            m = jnp.maximum(jnp.max(sl, axis=1, keepdims=True), jnp.max(sc, axis=1, keepdims=True))
            e = jnp.exp(sl - m)
            ec = jnp.exp(sc - m)
            denom = jnp.sum(e, axis=1, keepdims=True) + jnp.sum(ec, axis=1, keepdims=True)
            p_ref[rows, cols] = e.astype(BF16)
            pc_ref[rows, :] = ec.astype(BF16)
            inv_ref[rows, :] = 1.0 / denom

    half = B_WIN_ROWS // 2

    @pl.when(g == 0)
    def _():
        body(0, lambda i: max(i - half, 0))

    @pl.when(jnp.logical_and(g > 0, g < n_groups - 1))
    def _():
        body(-half, lambda i: i)

    @pl.when(g == n_groups - 1)
    def _():
        body(-(B_KROWS - B_GROUP), lambda i: min(i + half, B_KROWS - B_WIN_ROWS))

    o = (jnp.dot(p_ref[...], v, preferred_element_type=F32)
         + jnp.dot(pc_ref[...], vc_ref[...], preferred_element_type=F32))
    o_ref[...] = (o * inv_ref[...]).astype(BF16)


def _attn_b(l, p, tables):
    n_groups = (S // GRID_W) // B_GROUP
    return pl.pallas_call(
        _attn_b_kernel,
        grid=(B_HEADS, NB, n_groups),
        in_specs=[
            pl.BlockSpec((B_Q, HD), lambda h, b, g: (b * n_groups + g, OFF_QB // HD + h)),
            pl.BlockSpec((S, HD), lambda h, b, g: (b, OFF_KB // HD + h)),
            pl.BlockSpec((S, HD), lambda h, b, g: (b, OFF_VB // HD + h)),
            pl.BlockSpec((CTX, HD), lambda h, b, g: (T_LAT // CTX + b, OFF_KB // HD + h)),
            pl.BlockSpec((CTX, HD), lambda h, b, g: (T_LAT // CTX + b, OFF_VB // HD + h)),
            pl.BlockSpec((None, None, B_NTAB, GRID_W, 2 * GRID_W), lambda h, b, g: (l, h, 0, 0, 0)),
        ],
        out_specs=pl.BlockSpec((B_Q, HD), lambda h, b, g: (b * n_groups + g, h)),
        out_shape=jax.ShapeDtypeStruct((T_LAT, B_HEADS * HD), BF16),
        scratch_shapes=[
            pltpu.VMEM((B_Q, B_K), F32),
            pltpu.VMEM((B_Q, B_K), BF16),
            pltpu.VMEM((B_Q, CTX), BF16),
            pltpu.VMEM((B_Q, 1), F32),
        ],
        compiler_params=_cparams("arbitrary", "arbitrary", "arbitrary"),
        name="attn_b",
    )(p, p, p, p, p, tables)


def _ctx_attn_kernel(sink_ref, q_ref, k_ref, v_ref, o_ref, *, rep, has_sink):
    g = pl.program_id(1)
    qs = _stack_heads(q_ref[...], rep)
    s = _dot_nt(qs, k_ref[...]) * SCALE
    m = jnp.max(s, axis=1, keepdims=True)
    rows = rep * CTX
    if has_sink:
        ridx = lax.broadcasted_iota(jnp.int32, (rows, 1), 0)
        sink = jnp.full((rows, 1), sink_ref[g * rep], F32)
        for r in range(1, rep):
            sink = jnp.where(ridx >= r * CTX, sink_ref[g * rep + r], sink)
        m = jnp.maximum(m, sink)
    e = jnp.exp(s - m)
    denom = jnp.sum(e, axis=1, keepdims=True)
    if has_sink:
        denom = denom + jnp.exp(sink - m)
    o = jnp.dot(e.astype(BF16), v_ref[...], preferred_element_type=F32) / denom
    for r in range(rep):
        o_ref[:, r * HD:(r + 1) * HD] = o[r * CTX:(r + 1) * CTX, :].astype(BF16)


def _ctx_attn(p, sink_l, *, off_q, off_k, off_v, n_kv, rep, has_sink):
    qw = rep * HD
    rb = T_LAT // CTX
    kern = functools.partial(_ctx_attn_kernel, rep=rep, has_sink=has_sink)
    return pl.pallas_call(
        kern,
        grid_spec=pltpu.PrefetchScalarGridSpec(
            num_scalar_prefetch=1,
            grid=(NB, n_kv),
            in_specs=[
                pl.BlockSpec((CTX, qw), lambda b, g, s: (rb + b, off_q // qw + g)),
                pl.BlockSpec((CTX, HD), lambda b, g, s: (rb + b, off_k // HD + g)),
                pl.BlockSpec((CTX, HD), lambda b, g, s: (rb + b, off_v // HD + g)),
            ],
            out_specs=pl.BlockSpec((CTX, qw), lambda b, g, s: (b, g)),
        ),
        out_shape=jax.ShapeDtypeStruct((T_CTX, n_kv * qw), BF16),
        compiler_params=_cparams("arbitrary", "arbitrary"),
        name="ctx_attn_a" if has_sink else "ctx_attn_b",
    )(sink_l, p, p, p)


TM_M = 256
Z_COLS = OFF_VB - OFF_ZA


def _mix_kernel(*refs, has_ctx):
    if has_ctx:
        a_ref, b_ref, ac_ref, bc_ref = refs[:4]
        refs = refs[4:]
    else:
        a_ref, b_ref = refs[:2]
        refs = refs[2:]
    c_ref, cp_ref, cn_ref, z_ref, x_ref, gate_ref, wa_ref, wb_ref, wc_ref, wo_ref, cw_ref, o_ref = refs
    i = pl.program_id(0)
    is_lat = i < T_LAT // TM_M
    if has_ctx:
        a = jnp.where(is_lat, a_ref[...], ac_ref[...])
        b = jnp.where(is_lat, b_ref[...], bc_ref[...])
    else:
        a, b = a_ref[...], b_ref[...]

    w = C_WIDTH

    def gated(ref):
        return ref[:, w:2 * w].astype(F32) * ref[:, 0:w].astype(F32)

    v = gated(c_ref)
    vp = gated(cp_ref)[HALO - 1:HALO, :]
    vn = gated(cn_ref)[0:1, :]
    row = lax.broadcasted_iota(jnp.int32, (TM_M, w), 0)
    v_prev = jnp.where(row == 0, vp, pltpu.roll(v, 1, axis=0))
    v_next = jnp.where(row == TM_M - 1, vn, pltpu.roll(v, TM_M - 1, axis=0))
    lmask = jnp.where(is_lat, S - 1, CTX - 1)
    pos = (i * TM_M + row) & lmask
    cw = cw_ref[...]
    conv = (cw[0:1, :] * jnp.where(pos != 0, v_prev, 0.0) + cw[1:2, :] * v
            + cw[2:3, :] * jnp.where(pos != lmask, v_next, 0.0))
    cb = (c_ref[:, 2 * w:3 * w].astype(F32) * conv).astype(BF16)

    ya = jnp.dot(a, wa_ref[...], preferred_element_type=F32)
    yb = jnp.dot(b, wb_ref[...], preferred_element_type=F32)
    yc = jnp.dot(cb, wc_ref[...], preferred_element_type=F32)
    m = (jax.nn.sigmoid(z_ref[:, 0:D].astype(F32)) * ya + jax.nn.sigmoid(z_ref[:, D:2 * D].astype(F32)) * yb
         + jax.nn.sigmoid(z_ref[:, 2 * D:3 * D].astype(F32)) * yc)
    y = jnp.dot(m.astype(BF16), wo_ref[...], preferred_element_type=F32)
    o_ref[...] = x_ref[...] + gate_ref[...] * y


def _mix(l, with_ctx, attn_lat, attn_ctx, p, xall, mods5, w_pa, w_pb, w_pc, w_o, conv_c):
    cw3 = 3 * C_WIDTH
    hb = TM_M // HALO
    last_hb = T_ALL // HALO - 1
    lat_tiles = T_LAT // TM_M
    n_tiles = (T_ALL if with_ctx else T_LAT) // TM_M
    aw, bw = A_HEADS * HD, B_HEADS * HD
    attn_specs = [pl.BlockSpec((TM_M, aw), lambda i: (jnp.minimum(i, lat_tiles - 1), 0)),
                  pl.BlockSpec((TM_M, bw), lambda i: (jnp.minimum(i, lat_tiles - 1), 0))]
    attn_args = list(attn_lat)
    if with_ctx:
        attn_specs += [pl.BlockSpec((TM_M, aw), lambda i: (jnp.maximum(i - lat_tiles, 0), 0)),
                       pl.BlockSpec((TM_M, bw), lambda i: (jnp.maximum(i - lat_tiles, 0), 0))]
        attn_args += list(attn_ctx)

    def resident(shape):
        return pl.BlockSpec((None,) + shape, lambda i: (l, 0, 0), pipeline_mode=pl.Buffered(1))

    return pl.pallas_call(
        functools.partial(_mix_kernel, has_ctx=with_ctx),
        grid=(n_tiles,),
        in_specs=attn_specs + [
            pl.BlockSpec((TM_M, cw3), lambda i: (i, OFF_UC // cw3)),
            pl.BlockSpec((HALO, cw3), lambda i: (jnp.maximum(i * hb - 1, 0), OFF_UC // cw3)),
            pl.BlockSpec((HALO, cw3), lambda i: (jnp.minimum((i + 1) * hb, last_hb), OFF_UC // cw3)),
            pl.BlockSpec((TM_M, Z_COLS), lambda i: (i, OFF_ZA // Z_COLS)),
            pl.BlockSpec((TM_M, D), lambda i: (i, 0)),
            pl.BlockSpec((None, None, None, 1, D), lambda i: (l, _mod_row(i, TM_M), 2, 0, 0)),
            resident((aw, D)),
            resident((bw, D)),
            resident((C_WIDTH, D)),
            resident((D, D)),
            pl.BlockSpec((None, 3, C_WIDTH), lambda i: (l, 0, 0)),
        ],
        out_specs=pl.BlockSpec((TM_M, D), lambda i: (i, 0)),
        out_shape=jax.ShapeDtypeStruct((n_tiles * TM_M, D), F32),
        compiler_params=_cparams("arbitrary"),
        name="mix",
    )(*attn_args, p, p, p, p, xall, mods5, w_pa, w_pb, w_pc, w_o, conv_c)


TM_F = 1024
CH = 256
N_CH = D_FF // CH
ROWS_PER_PIECE = 64


def _chunk_major(w):
    r = w.shape[1]
    w = w.reshape(DEPTH, r, 2, N_CH, CH)
    return jnp.transpose(w, (0, 3, 1, 2, 4)).reshape(DEPTH, N_CH, r, 2 * CH)


def _ffn_kernel(x_ref, xp_ref, xn_ref, g_ref, shift_ref, scale_ref, gate_ref,
                wu_ref, cw_ref, wd_ref, nf_ref, o_ref, h_ref, act0_ref, act1_ref, u0_ref, u1_ref,
                *, final_norm, lat_tiles):
    i = pl.program_id(0)
    j = pl.program_id(1)
    nj = N_CH
    n_slab = 2 * CH // HD
    half = n_slab // 2

    @pl.when(j == 0)
    def _():
        g, sh, sc = g_ref[...], shift_ref[...], scale_ref[...]
        h_ref[0:HALO, :] = _norm_mod(xp_ref[...], g, sh, sc).astype(BF16)
        h_ref[HALO:HALO + TM_F, :] = _norm_mod(x_ref[...], g, sh, sc).astype(BF16)
        h_ref[HALO + TM_F:, :] = _norm_mod(xn_ref[...], g, sh, sc).astype(BF16)
        o_ref[...] = jnp.zeros((TM_F, D), F32)

    act_ref = (act0_ref, act1_ref)
    u_ref = (u0_ref, u1_ref)

    def up(slot):
        u = jnp.dot(h_ref[...], wu_ref[...], preferred_element_type=F32)
        for c in range(n_slab):
            u_ref[slot][c] = u[:, c * HD:(c + 1) * HD]

    def epilogue(slot):
        rb = ROWS_PER_PIECE
        lmask = jnp.where(i < lat_tiles, S - 1, CTX - 1)

        def conv(c, r0, has_prev, has_next):
            u_prev = jnp.where(has_prev, u_ref[slot][c, HALO - 1 + r0:HALO - 1 + r0 + rb, :], 0.0)
            u_next = jnp.where(has_next, u_ref[slot][c, HALO + 1 + r0:HALO + 1 + r0 + rb, :], 0.0)
            cw = cw_ref[:, c * HD:(c + 1) * HD]
            return (cw[0:1, :] * u_prev + cw[1:2, :] * u_ref[slot][c, HALO + r0:HALO + r0 + rb, :]
                    + cw[2:3, :] * u_next)

        for r0 in range(0, TM_F, rb):
            pos = (i * TM_F + r0 + lax.broadcasted_iota(jnp.int32, (rb, HD), 0)) & lmask
            has_prev, has_next = pos != 0, pos != lmask
            for c in range(half):
                cgate = conv(c, r0, has_prev, has_next)
                cval = conv(half + c, r0, has_prev, has_next)
                act_ref[slot][r0:r0 + rb, c * HD:(c + 1) * HD] = (
                    cgate * jax.nn.sigmoid(cgate) * cval).astype(BF16)

    def down(slot):
        return jnp.dot(act_ref[slot][...], wd_ref[...], preferred_element_type=F32)

    @pl.when(j == 0)
    def _():
        up(0)

    @pl.when(j == 1)
    def _():
        epilogue(0)
        up(1)

    for parity in (0, 1):
        @pl.when(jnp.logical_and(jnp.logical_and(j >= 2, j < nj), j % 2 == parity))
        def _():
            epilogue(1 - parity)
            up(parity)
            o_ref[...] += down(parity)

    @pl.when(j == nj)
    def _():
        epilogue((nj - 1) % 2)
        o_ref[...] += down(nj % 2)

    @pl.when(j == nj + 1)
    def _():
        xo = x_ref[...] + gate_ref[...] * (o_ref[...] + down((nj - 1) % 2))
        if final_norm:
            ms = jnp.mean(xo * xo, axis=-1, keepdims=True)
            xo = xo * lax.rsqrt(ms + EPS) * nf_ref[...]
        o_ref[...] = xo


def _ffn(l, with_ctx, xall, norm2, mods5, w_up_c, conv_f_c, w_down, norm_f, *, final_norm):
    hb = TM_F // HALO
    last_hb = xall.shape[0] // HALO - 1
    lat_tiles = T_LAT // TM_F
    n_tiles = (T_ALL if with_ctx else T_LAT) // TM_F
    kern = functools.partial(_ffn_kernel, final_norm=final_norm, lat_tiles=lat_tiles)

    def chunk(j, lag):
        return jnp.clip(j - lag, 0, N_CH - 1)

    return pl.pallas_call(
        kern,
        grid=(n_tiles, N_CH + 2),
        in_specs=[
            pl.BlockSpec((TM_F, D), lambda i, j: (i, 0)),
            pl.BlockSpec((HALO, D), lambda i, j: (jnp.maximum(i * hb - 1, 0), 0)),
            pl.BlockSpec((HALO, D), lambda i, j: (jnp.minimum((i + 1) * hb, last_hb), 0)),
            pl.BlockSpec((None, 1, D), lambda i, j: (l, 0, 0)),
            _mod_spec(l, 3, TM_F),
            _mod_spec(l, 4, TM_F),
            _mod_spec(l, 5, TM_F),
            pl.BlockSpec((None, None, D, 2 * CH), lambda i, j: (l, chunk(j, 0), 0, 0)),
            pl.BlockSpec((None, None, 3, 2 * CH), lambda i, j: (l, chunk(j, 1), 0, 0)),
            pl.BlockSpec((None, CH, D), lambda i, j: (l, chunk(j, 2), 0)),
            pl.BlockSpec((1, D), lambda i, j: (0, 0)),
        ],
        out_specs=pl.BlockSpec((TM_F, D), lambda i, j: (i, 0)),
        out_shape=jax.ShapeDtypeStruct((n_tiles * TM_F, D), F32),
        scratch_shapes=([pltpu.VMEM((TM_F + 2 * HALO, D), BF16)] + 2 * [pltpu.VMEM((TM_F, CH), BF16)]
                        + 2 * [pltpu.VMEM((2 * CH // HD, TM_F + 2 * HALO, HD), F32)]),
        compiler_params=_cparams("arbitrary", "arbitrary"),
        name="ffn_final" if final_norm else "ffn",
    )(xall, xall, xall, norm2.reshape(DEPTH, 1, D), mods5, mods5, mods5,
      w_up_c, conv_f_c, w_down, norm_f.reshape(1, D))


def _rope_tables():
    t = jnp.arange(S)
    row = (t // GRID_W).astype(F32)
    col = (t % GRID_W).astype(F32)
    n_freq = HD // 4
    inv = 1.0 / (ROPE_THETA ** (jnp.arange(n_freq, dtype=F32) / n_freq))
    ar = row[:, None] * inv
    ac = col[:, None] * inv
    ang = jnp.concatenate([ar, ar, ac, ac], axis=-1)
    lane = jnp.arange(HD)
    sign = jnp.where((lane & 32) == 0, -1.0, 1.0).astype(F32)
    return jnp.cos(ang), jnp.sin(ang) * sign


def _permute_in_cols(w):
    return jnp.concatenate([
        w[..., 0:1024],
        w[..., 1536:2560],
        w[..., 2560:3584],
        w[..., 4608:7680],
        w[..., 7680:13824],
        w[..., 3584:4608],
        w[..., 1024:1536],
    ], axis=-1)


def kernel(x, c, ctx, c_ctx, w_mod, b_mod, norm1, w_in, sink, rpb, conv_c, w_pa, w_pb, w_pc, w_o,
           norm2, w_up, conv_f, w_down, norm_f):
    c8 = jnp.concatenate([c, c_ctx[None, :], jnp.zeros((8 - NB - 1, D), F32)], axis=0)
    mods = _mods(c8, w_mod, b_mod)
    mods5 = mods.reshape(DEPTH, 8, N_MOD, 1, D)
    cos, sin_signed = _rope_tables()
    tables = _bias_tables(rpb)
    w_in_b = _permute_in_cols(w_in).astype(BF16)
    w_pa_b, w_pb_b, w_pc_b = w_pa.astype(BF16), w_pb.astype(BF16), w_pc.astype(BF16)
    w_o_b, w_down_b = w_o.astype(BF16), w_down.astype(BF16)
    w_up_c, conv_f_c = _chunk_major(w_up.astype(BF16)), _chunk_major(conv_f)

    xall = jnp.concatenate([x.reshape(T_LAT, D), ctx.reshape(T_CTX, D)], axis=0)
    for l in range(DEPTH):
        last = l == DEPTH - 1
        p = _inproj(l, xall, norm1, mods5, w_in_b, cos, sin_signed)
        attn_lat = (_attn_a(p, sink[l]), _attn_b(l, p, tables))
        attn_ctx = None
        if not last:
            attn_ctx = (_ctx_attn(p, sink[l], off_q=OFF_QA, off_k=OFF_KA, off_v=OFF_VA,
                                  n_kv=A_KV, rep=A_REP, has_sink=True),
                        _ctx_attn(p, sink[l], off_q=OFF_QB, off_k=OFF_KB, off_v=OFF_VB,
                                  n_kv=B_HEADS, rep=1, has_sink=False))
        xall = _mix(l, not last, attn_lat, attn_ctx, p, xall, mods5, w_pa_b, w_pb_b, w_pc_b, w_o_b, conv_c)
        xall = _ffn(l, not last, xall, norm2, mods5, w_up_c, conv_f_c, w_down_b, norm_f, final_norm=last)
    return xall.reshape(NB, S, D)
```

```python
import functools

import numpy as np
import jax
import jax.numpy as jnp
from jax import lax
from jax.experimental import pallas as pl
from jax.experimental.pallas import tpu as pltpu

F32 = jnp.float32
BF16 = jnp.bfloat16

D = 2048
NB = 4
S = 4096
DEPTH = 4
GRID_W = 64
CTX = 256
HD = 128
A_HEADS = 8
A_KV = 2
A_REP = A_HEADS // A_KV
A_WIN = 128
A_BLOCK = 128
B_HEADS = 8
B_WIN_ROWS = 8
B_WIN_COLS = 16
C_WIDTH = 1024
D_FF = 5632
N_MOD = 6
ROPE_THETA = 10000.0
EPS = 1e-6
NEG = -1e30
SCALE = HD ** -0.5

T_LAT = NB * S
T_CTX = NB * CTX
T_ALL = T_LAT + T_CTX

OFF_QA = 0
OFF_QB = 1024
OFF_KB = 2048
OFF_UC = 3072
OFF_ZA = 6144
OFF_ZB = 8192
OFF_ZC = 10240
OFF_VB = 12288
OFF_KA = 13312
OFF_VA = 13568
IN_COLS = 13824

HALO = 8

VMEM_LIMIT = 56 * 1024 * 1024


def _cparams(*sem):
    return pltpu.CompilerParams(dimension_semantics=sem, vmem_limit_bytes=VMEM_LIMIT)


def _mod_row(i, tm):
    return jnp.minimum(i // (S // tm), NB)


def _mod_spec(l, k, tm):
    return pl.BlockSpec((None, None, None, 1, D), lambda i, j: (l, _mod_row(i, tm), k, 0, 0))


def _mods_kernel(c_ref, w_ref, b_ref, o_ref):
    cv = c_ref[...]
    sc = cv * jax.nn.sigmoid(cv)
    o_ref[...] = jnp.dot(sc, w_ref[...], preferred_element_type=F32,
                         precision=lax.Precision.HIGHEST) + b_ref[...]


def _mods(c8, w_mod, b_mod):
    tn = 1024
    ncols = N_MOD * D
    return pl.pallas_call(
        _mods_kernel,
        grid=(DEPTH, ncols // tn),
        in_specs=[
            pl.BlockSpec((8, D), lambda l, j: (0, 0)),
            pl.BlockSpec((None, D, tn), lambda l, j: (l, 0, j)),
            pl.BlockSpec((None, 1, tn), lambda l, j: (l, 0, j)),
        ],
        out_specs=pl.BlockSpec((None, 8, tn), lambda l, j: (l, 0, j)),
        out_shape=jax.ShapeDtypeStruct((DEPTH, 8, ncols), F32),
        compiler_params=_cparams("arbitrary", "arbitrary"),
        name="mods",
    )(c8, w_mod, b_mod.reshape(DEPTH, 1, ncols))


def _norm_mod(x, g, shift, scale):
    ms = jnp.mean(x * x, axis=-1, keepdims=True)
    y = x * lax.rsqrt(ms + EPS)
    y = y * g
    return y * (1.0 + scale) + shift


def _rope(u, cos, sin_signed, swap_lo):
    rot = jnp.where(swap_lo, pltpu.roll(u, 96, axis=1), pltpu.roll(u, 32, axis=1))
    return u * cos + rot * sin_signed


TM_IN = 1024
TN_IN = 1536
ROPE_CHUNKS = {
    OFF_QA // TN_IN: tuple(range((OFF_QB - OFF_QA) // HD)),
    OFF_KA // TN_IN: tuple((OFF_KA % TN_IN) // HD + c for c in range((OFF_VA - OFF_KA) // HD)),
}


def _inproj_kernel(x_ref, g_ref, shift_ref, scale_ref, w_ref, cos_ref, sin_ref, o_ref, h_ref):
    i = pl.program_id(0)
    j = pl.program_id(1)

    @pl.when(j == 0)
    def _():
        h = _norm_mod(x_ref[...], g_ref[...], shift_ref[...], scale_ref[...])
        h_ref[...] = h.astype(BF16)

    acc = jnp.dot(h_ref[...], w_ref[...], preferred_element_type=F32)
    o_ref[...] = acc.astype(BF16)

    for tile, chunks in ROPE_CHUNKS.items():
        @pl.when(jnp.logical_and(i < T_LAT // TM_IN, j == tile))
        def _():
            lane = lax.broadcasted_iota(jnp.int32, (TM_IN, HD), 1)
            swap_lo = (lane & 32) == 0
            for ch in chunks:
                cols = slice(ch * HD, (ch + 1) * HD)
                o_ref[:, cols] = _rope(acc[:, cols], cos_ref[...], sin_ref[...], swap_lo).astype(BF16)


def _inproj(l, xall, norm1, mods5, w_in, cos, sin_signed):
    tiles_per_seq = S // TM_IN
    return pl.pallas_call(
        _inproj_kernel,
        grid=(T_ALL // TM_IN, IN_COLS // TN_IN),
        in_specs=[
            pl.BlockSpec((TM_IN, D), lambda i, j: (i, 0)),
            pl.BlockSpec((None, 1, D), lambda i, j: (l, 0, 0)),
            _mod_spec(l, 0, TM_IN),
            _mod_spec(l, 1, TM_IN),
            pl.BlockSpec((None, D, TN_IN), lambda i, j: (l, 0, j)),
            pl.BlockSpec((TM_IN, HD), lambda i, j: (i % tiles_per_seq, 0)),
            pl.BlockSpec((TM_IN, HD), lambda i, j: (i % tiles_per_seq, 0)),
        ],
        out_specs=pl.BlockSpec((TM_IN, TN_IN), lambda i, j: (i, j)),
        out_shape=jax.ShapeDtypeStruct((T_ALL, IN_COLS), BF16),
        scratch_shapes=[pltpu.VMEM((TM_IN, D), BF16)],
        compiler_params=_cparams("arbitrary", "arbitrary"),
        name="inproj",
    )(xall, norm1.reshape(DEPTH, 1, D), mods5, mods5, w_in, cos, sin_signed)


A_KEYS = 3 * A_BLOCK


def _stack_heads(q, rep):
    if rep == 1:
        return q
    return jnp.concatenate([q[:, r * HD:(r + 1) * HD] for r in range(rep)], axis=0)


def _dot_nt(a, b):
    return lax.dot_general(a, b, (((1,), (1,)), ((), ())), preferred_element_type=F32)


def _attn_a_kernel(sink_ref, q_ref, k_ref, v_ref, kc_ref, vc_ref, o_ref):
    g = pl.program_id(1)
    n = pl.program_id(2)
    q4 = _stack_heads(q_ref[...], A_REP)
    start = pl.multiple_of(jnp.clip(n * A_BLOCK - A_BLOCK, 0, S - A_KEYS), A_BLOCK)
    k = k_ref[pl.ds(start, A_KEYS), :]
    v = v_ref[pl.ds(start, A_KEYS), :]
    rows = A_REP * A_BLOCK
    s_loc = _dot_nt(q4, k) * SCALE
    s_ctx = _dot_nt(q4, kc_ref[...]) * SCALE
    qpos = n * A_BLOCK + (lax.broadcasted_iota(jnp.int32, (rows, A_KEYS), 0) & (A_BLOCK - 1))
    kpos = start + lax.broadcasted_iota(jnp.int32, (rows, A_KEYS), 1)
    s_loc = jnp.where(jnp.abs(kpos - qpos) <= A_WIN, s_loc, NEG)
    ridx = lax.broadcasted_iota(jnp.int32, (rows, 1), 0)
    sink = jnp.full((rows, 1), sink_ref[g * A_REP], F32)
    for r in range(1, A_REP):
        sink = jnp.where(ridx >= r * A_BLOCK, sink_ref[g * A_REP + r], sink)
    m = jnp.maximum(jnp.maximum(jnp.max(s_loc, axis=1, keepdims=True),
                                jnp.max(s_ctx, axis=1, keepdims=True)), sink)
    e_loc = jnp.exp(s_loc - m)
    e_ctx = jnp.exp(s_ctx - m)
    denom = (jnp.sum(e_loc, axis=1, keepdims=True) + jnp.sum(e_ctx, axis=1, keepdims=True)
             + jnp.exp(sink - m))
    o = (jnp.dot(e_loc.astype(BF16), v, preferred_element_type=F32)
         + jnp.dot(e_ctx.astype(BF16), vc_ref[...], preferred_element_type=F32))
    o = o / denom
    for r in range(A_REP):
        o_ref[:, r * HD:(r + 1) * HD] = o[r * A_BLOCK:(r + 1) * A_BLOCK, :].astype(BF16)


def _attn_a(p, sink_l):
    nblk = S // A_BLOCK
    qw = A_REP * HD
    return pl.pallas_call(
        _attn_a_kernel,
        grid_spec=pltpu.PrefetchScalarGridSpec(
            num_scalar_prefetch=1,
            grid=(NB, A_KV, nblk),
            in_specs=[
                pl.BlockSpec((A_BLOCK, qw), lambda b, g, n, s: (b * nblk + n, g)),
                pl.BlockSpec((S, HD), lambda b, g, n, s: (b, OFF_KA // HD + g)),
                pl.BlockSpec((S, HD), lambda b, g, n, s: (b, OFF_VA // HD + g)),
                pl.BlockSpec((CTX, HD), lambda b, g, n, s: (T_LAT // CTX + b, OFF_KA // HD + g)),
                pl.BlockSpec((CTX, HD), lambda b, g, n, s: (T_LAT // CTX + b, OFF_VA // HD + g)),
            ],
            out_specs=pl.BlockSpec((A_BLOCK, qw), lambda b, g, n, s: (b * nblk + n, g)),
        ),
        out_shape=jax.ShapeDtypeStruct((T_LAT, A_HEADS * HD), BF16),
        compiler_params=_cparams("arbitrary", "arbitrary", "arbitrary"),
        name="attn_a",
    )(sink_l, p, p, p, p, p)


B_GROUP = 8
B_Q = B_GROUP * GRID_W
B_KROWS = 16
B_K = B_KROWS * GRID_W
B_NTAB = 2 * B_WIN_ROWS - 2


def _bias_tables(rpb):
    c = np.arange(GRID_W)[:, None]
    kc = np.arange(GRID_W)[None, :]
    cs = np.clip(c - B_WIN_COLS // 2, 0, GRID_W - B_WIN_COLS)
    ok = (kc >= cs) & (kc < cs + B_WIN_COLS)
    idx = np.clip(kc - c, 1 - B_WIN_COLS, B_WIN_COLS - 1) + B_WIN_COLS - 1
    mt = jnp.where(jnp.asarray(ok), rpb[..., idx].astype(F32), NEG)
    return jnp.concatenate([mt[:, :, :-1], mt[:, :, 1:]], axis=-1)


def _attn_b_kernel(q_ref, k_ref, v_ref, kc_ref, vc_ref, t_ref, o_ref, s_ref, p_ref, oloc_ref, pc_ref, inv_ref):
    g = pl.program_id(2)
    n_groups = (S // GRID_W) // B_GROUP
    r0 = g * B_GROUP
    base = jnp.clip(r0 - B_WIN_ROWS // 2, 0, S // GRID_W - B_KROWS)
    kstart = base * GRID_W
    win = B_WIN_ROWS * GRID_W
    sc_all = _dot_nt(q_ref[...], kc_ref[...]) * SCALE

    def body(off, jlo_of):
        def window(ref, i):
            k0 = pl.multiple_of(kstart + jlo_of(i) * GRID_W, GRID_W)
            return ref[pl.ds(k0, win), :]

        for i in range(B_GROUP):
            rows = slice(i * GRID_W, (i + 1) * GRID_W)
            s_ref[rows, :] = _dot_nt(q_ref[rows, :], window(k_ref, i))
        for i in range(B_GROUP):
            rows = slice(i * GRID_W, (i + 1) * GRID_W)
            bias = jnp.concatenate(
                [t_ref[off + jlo_of(i) + 2 * pp - i + B_WIN_ROWS - 1] for pp in range(B_WIN_ROWS // 2)],
                axis=1)
            sl = s_ref[rows, :] * SCALE + bias
            sc = sc_all[rows, :]
            m = jnp.maximum(jnp.max(sl, axis=1, keepdims=True), jnp.max(sc, axis=1, keepdims=True))
            e = jnp.exp(sl - m)
            ec = jnp.exp(sc - m)
            denom = jnp.sum(e, axis=1, keepdims=True) + jnp.sum(ec, axis=1, keepdims=True)
            p_ref[rows, :] = e.astype(BF16)
            pc_ref[rows, :] = ec.astype(BF16)
            inv_ref[rows, :] = 1.0 / denom
        for i in range(B_GROUP):
            rows = slice(i * GRID_W, (i + 1) * GRID_W)
            oloc_ref[rows, :] = jnp.dot(p_ref[rows, :], window(v_ref, i), preferred_element_type=F32)

    half = B_WIN_ROWS // 2

    @pl.when(g == 0)
    def _():
        body(0, lambda i: max(i - half, 0))

    @pl.when(jnp.logical_and(g > 0, g < n_groups - 1))
    def _():
        body(-half, lambda i: i)

    @pl.when(g == n_groups - 1)
    def _():
        body(-(B_KROWS - B_GROUP), lambda i: min(i + half, B_KROWS - B_WIN_ROWS))

    o = oloc_ref[...] + jnp.dot(pc_ref[...], vc_ref[...], preferred_element_type=F32)
    o_ref[...] = (o * inv_ref[...]).astype(BF16)


def _attn_b(l, p, tables):
    n_groups = (S // GRID_W) // B_GROUP
    return pl.pallas_call(
        _attn_b_kernel,
        grid=(B_HEADS, NB, n_groups),
        in_specs=[
            pl.BlockSpec((B_Q, HD), lambda h, b, g: (b * n_groups + g, OFF_QB // HD + h)),
            pl.BlockSpec((S, HD), lambda h, b, g: (b, OFF_KB // HD + h)),
            pl.BlockSpec((S, HD), lambda h, b, g: (b, OFF_VB // HD + h)),
            pl.BlockSpec((CTX, HD), lambda h, b, g: (T_LAT // CTX + b, OFF_KB // HD + h)),
            pl.BlockSpec((CTX, HD), lambda h, b, g: (T_LAT // CTX + b, OFF_VB // HD + h)),
            pl.BlockSpec((None, None, B_NTAB, GRID_W, 2 * GRID_W), lambda h, b, g: (l, h, 0, 0, 0)),
        ],
        out_specs=pl.BlockSpec((B_Q, HD), lambda h, b, g: (b * n_groups + g, h)),
        out_shape=jax.ShapeDtypeStruct((T_LAT, B_HEADS * HD), BF16),
        scratch_shapes=[
            pltpu.VMEM((B_Q, B_WIN_ROWS * GRID_W), F32),
            pltpu.VMEM((B_Q, B_WIN_ROWS * GRID_W), BF16),
            pltpu.VMEM((B_Q, HD), F32),
            pltpu.VMEM((B_Q, CTX), BF16),
            pltpu.VMEM((B_Q, 1), F32),
        ],
        compiler_params=_cparams("arbitrary", "arbitrary", "arbitrary"),
        name="attn_b",
    )(p, p, p, p, p, tables)


def _ctx_attn_kernel(sink_ref, q_ref, k_ref, v_ref, o_ref, *, rep, has_sink):
    g = pl.program_id(1)
    qs = _stack_heads(q_ref[...], rep)
    s = _dot_nt(qs, k_ref[...]) * SCALE
    m = jnp.max(s, axis=1, keepdims=True)
    rows = rep * CTX
    if has_sink:
        ridx = lax.broadcasted_iota(jnp.int32, (rows, 1), 0)
        sink = jnp.full((rows, 1), sink_ref[g * rep], F32)
        for r in range(1, rep):
            sink = jnp.where(ridx >= r * CTX, sink_ref[g * rep + r], sink)
        m = jnp.maximum(m, sink)
    e = jnp.exp(s - m)
    denom = jnp.sum(e, axis=1, keepdims=True)
    if has_sink:
        denom = denom + jnp.exp(sink - m)
    o = jnp.dot(e.astype(BF16), v_ref[...], preferred_element_type=F32) / denom
    for r in range(rep):
        o_ref[:, r * HD:(r + 1) * HD] = o[r * CTX:(r + 1) * CTX, :].astype(BF16)


def _ctx_attn(p, sink_l, *, off_q, off_k, off_v, n_kv, rep, has_sink):
    qw = rep * HD
    rb = T_LAT // CTX
    kern = functools.partial(_ctx_attn_kernel, rep=rep, has_sink=has_sink)
    return pl.pallas_call(
        kern,
        grid_spec=pltpu.PrefetchScalarGridSpec(
            num_scalar_prefetch=1,
            grid=(NB, n_kv),
            in_specs=[
                pl.BlockSpec((CTX, qw), lambda b, g, s: (rb + b, off_q // qw + g)),
                pl.BlockSpec((CTX, HD), lambda b, g, s: (rb + b, off_k // HD + g)),
                pl.BlockSpec((CTX, HD), lambda b, g, s: (rb + b, off_v // HD + g)),
            ],
            out_specs=pl.BlockSpec((CTX, qw), lambda b, g, s: (b, g)),
        ),
        out_shape=jax.ShapeDtypeStruct((T_CTX, n_kv * qw), BF16),
        compiler_params=_cparams("arbitrary", "arbitrary"),
        name="ctx_attn_a" if has_sink else "ctx_attn_b",
    )(sink_l, p, p, p)


TM_M = 256
Z_COLS = OFF_VB - OFF_ZA


def _mix_kernel(*refs, has_ctx):
    if has_ctx:
        a_ref, b_ref, ac_ref, bc_ref = refs[:4]
        refs = refs[4:]
    else:
        a_ref, b_ref = refs[:2]
        refs = refs[2:]
    c_ref, cp_ref, cn_ref, z_ref, x_ref, gate_ref, wa_ref, wb_ref, wc_ref, wo_ref, cw_ref, o_ref = refs
    i = pl.program_id(0)
    is_lat = i < T_LAT // TM_M
    if has_ctx:
        a = jnp.where(is_lat, a_ref[...], ac_ref[...])
        b = jnp.where(is_lat, b_ref[...], bc_ref[...])
    else:
        a, b = a_ref[...], b_ref[...]

    w = C_WIDTH

    def gated(ref):
        return ref[:, w:2 * w].astype(F32) * ref[:, 0:w].astype(F32)

    v = gated(c_ref)
    vp = gated(cp_ref)[HALO - 1:HALO, :]
    vn = gated(cn_ref)[0:1, :]
    row = lax.broadcasted_iota(jnp.int32, (TM_M, w), 0)
    v_prev = jnp.where(row == 0, vp, pltpu.roll(v, 1, axis=0))
    v_next = jnp.where(row == TM_M - 1, vn, pltpu.roll(v, TM_M - 1, axis=0))
    lmask = jnp.where(is_lat, S - 1, CTX - 1)
    pos = (i * TM_M + row) & lmask
    cw = cw_ref[...]
    conv = (cw[0:1, :] * jnp.where(pos != 0, v_prev, 0.0) + cw[1:2, :] * v
            + cw[2:3, :] * jnp.where(pos != lmask, v_next, 0.0))
    cb = (c_ref[:, 2 * w:3 * w].astype(F32) * conv).astype(BF16)

    ya = jnp.dot(a, wa_ref[...], preferred_element_type=F32)
    yb = jnp.dot(b, wb_ref[...], preferred_element_type=F32)
    yc = jnp.dot(cb, wc_ref[...], preferred_element_type=F32)
    m = (jax.nn.sigmoid(z_ref[:, 0:D].astype(F32)) * ya + jax.nn.sigmoid(z_ref[:, D:2 * D].astype(F32)) * yb
         + jax.nn.sigmoid(z_ref[:, 2 * D:3 * D].astype(F32)) * yc)
    y = jnp.dot(m.astype(BF16), wo_ref[...], preferred_element_type=F32)
    o_ref[...] = x_ref[...] + gate_ref[...] * y


def _mix(l, with_ctx, attn_lat, attn_ctx, p, xall, mods5, w_pa, w_pb, w_pc, w_o, conv_c):
    cw3 = 3 * C_WIDTH
    hb = TM_M // HALO
    last_hb = T_ALL // HALO - 1
    lat_tiles = T_LAT // TM_M
    n_tiles = (T_ALL if with_ctx else T_LAT) // TM_M
    aw, bw = A_HEADS * HD, B_HEADS * HD
    attn_specs = [pl.BlockSpec((TM_M, aw), lambda i: (jnp.minimum(i, lat_tiles - 1), 0)),
                  pl.BlockSpec((TM_M, bw), lambda i: (jnp.minimum(i, lat_tiles - 1), 0))]
    attn_args = list(attn_lat)
    if with_ctx:
        attn_specs += [pl.BlockSpec((TM_M, aw), lambda i: (jnp.maximum(i - lat_tiles, 0), 0)),
                       pl.BlockSpec((TM_M, bw), lambda i: (jnp.maximum(i - lat_tiles, 0), 0))]
        attn_args += list(attn_ctx)

    def resident(shape):
        return pl.BlockSpec((None,) + shape, lambda i: (l, 0, 0), pipeline_mode=pl.Buffered(1))

    return pl.pallas_call(
        functools.partial(_mix_kernel, has_ctx=with_ctx),
        grid=(n_tiles,),
        in_specs=attn_specs + [
            pl.BlockSpec((TM_M, cw3), lambda i: (i, OFF_UC // cw3)),
            pl.BlockSpec((HALO, cw3), lambda i: (jnp.maximum(i * hb - 1, 0), OFF_UC // cw3)),
            pl.BlockSpec((HALO, cw3), lambda i: (jnp.minimum((i + 1) * hb, last_hb), OFF_UC // cw3)),
            pl.BlockSpec((TM_M, Z_COLS), lambda i: (i, OFF_ZA // Z_COLS)),
            pl.BlockSpec((TM_M, D), lambda i: (i, 0)),
            pl.BlockSpec((None, None, None, 1, D), lambda i: (l, _mod_row(i, TM_M), 2, 0, 0)),
            resident((aw, D)),
            resident((bw, D)),
            resident((C_WIDTH, D)),
            resident((D, D)),
            pl.BlockSpec((None, 3, C_WIDTH), lambda i: (l, 0, 0)),
        ],
        out_specs=pl.BlockSpec((TM_M, D), lambda i: (i, 0)),
        out_shape=jax.ShapeDtypeStruct((n_tiles * TM_M, D), F32),
        compiler_params=_cparams("arbitrary"),
        name="mix",
    )(*attn_args, p, p, p, p, xall, mods5, w_pa, w_pb, w_pc, w_o, conv_c)


TM_F = 1024
CH = 256
N_CH = D_FF // CH
ROWS_PER_PIECE = 64


def _chunk_major(w):
    r = w.shape[1]
    w = w.reshape(DEPTH, r, 2, N_CH, CH)
    return jnp.transpose(w, (0, 3, 1, 2, 4)).reshape(DEPTH, N_CH, r, 2 * CH)


def _chunk_up_kernel(g_ref, v_ref, o_ref):
    o_ref[:, 0:CH] = g_ref[...].astype(BF16)
    o_ref[:, CH:2 * CH] = v_ref[...].astype(BF16)


def _chunk_up(w_up):
    return pl.pallas_call(
        _chunk_up_kernel,
        grid=(DEPTH, N_CH),
        in_specs=[pl.BlockSpec((None, D, CH), lambda l, k: (l, 0, k)),
                  pl.BlockSpec((None, D, CH), lambda l, k: (l, 0, N_CH + k))],
        out_specs=pl.BlockSpec((None, None, D, 2 * CH), lambda l, k: (l, k, 0, 0)),
        out_shape=jax.ShapeDtypeStruct((DEPTH, N_CH, D, 2 * CH), BF16),
        compiler_params=_cparams("arbitrary", "arbitrary"),
        name="chunk_up",
    )(w_up, w_up)


def _ffn_kernel(x_ref, xp_ref, xn_ref, g_ref, shift_ref, scale_ref, gate_ref,
                wu_ref, cw_ref, wd_ref, nf_ref, o_ref, h_ref, act0_ref, act1_ref, u0_ref, u1_ref,
                *, final_norm, lat_tiles):
    i = pl.program_id(0)
    j = pl.program_id(1)
    nj = N_CH
    n_slab = 2 * CH // HD
    half = n_slab // 2

    @pl.when(j == 0)
    def _():
        g, sh, sc = g_ref[...], shift_ref[...], scale_ref[...]
        h_ref[0:HALO, :] = _norm_mod(xp_ref[...], g, sh, sc).astype(BF16)
        h_ref[HALO:HALO + TM_F, :] = _norm_mod(x_ref[...], g, sh, sc).astype(BF16)
        h_ref[HALO + TM_F:, :] = _norm_mod(xn_ref[...], g, sh, sc).astype(BF16)
        o_ref[...] = jnp.zeros((TM_F, D), F32)

    act_ref = (act0_ref, act1_ref)
    u_ref = (u0_ref, u1_ref)

    def up(slot):
        u = jnp.dot(h_ref[...], wu_ref[...], preferred_element_type=F32)
        for c in range(n_slab):
            u_ref[slot][c] = u[:, c * HD:(c + 1) * HD]

    def epilogue(slot):
        rb = ROWS_PER_PIECE
        lmask = jnp.where(i < lat_tiles, S - 1, CTX - 1)

        def conv(c, r0, has_prev, has_next):
            u_prev = jnp.where(has_prev, u_ref[slot][c, HALO - 1 + r0:HALO - 1 + r0 + rb, :], 0.0)
            u_next = jnp.where(has_next, u_ref[slot][c, HALO + 1 + r0:HALO + 1 + r0 + rb, :], 0.0)
            cw = cw_ref[:, c * HD:(c + 1) * HD]
            return (cw[0:1, :] * u_prev + cw[1:2, :] * u_ref[slot][c, HALO + r0:HALO + r0 + rb, :]
                    + cw[2:3, :] * u_next)

        for r0 in range(0, TM_F, rb):
            pos = (i * TM_F + r0 + lax.broadcasted_iota(jnp.int32, (rb, HD), 0)) & lmask
            has_prev, has_next = pos != 0, pos != lmask
            for c in range(half):
                cgate = conv(c, r0, has_prev, has_next)
                cval = conv(half + c, r0, has_prev, has_next)
                act_ref[slot][r0:r0 + rb, c * HD:(c + 1) * HD] = (
                    cgate * jax.nn.sigmoid(cgate) * cval).astype(BF16)

    def down(slot):
        return jnp.dot(act_ref[slot][...], wd_ref[...], preferred_element_type=F32)

    @pl.when(j == 0)
    def _():
        up(0)

    @pl.when(j == 1)
    def _():
        epilogue(0)
        up(1)

    for parity in (0, 1):
        @pl.when(jnp.logical_and(jnp.logical_and(j >= 2, j < nj), j % 2 == parity))
        def _():
            epilogue(1 - parity)
            up(parity)
            o_ref[...] += down(parity)

    @pl.when(j == nj)
    def _():
        epilogue((nj - 1) % 2)
        o_ref[...] += down(nj % 2)

    @pl.when(j == nj + 1)
    def _():
        xo = x_ref[...] + gate_ref[...] * (o_ref[...] + down((nj - 1) % 2))
        if final_norm:
            ms = jnp.mean(xo * xo, axis=-1, keepdims=True)
            xo = xo * lax.rsqrt(ms + EPS) * nf_ref[...]
        o_ref[...] = xo


def _ffn(l, with_ctx, xall, norm2, mods5, w_up_c, conv_f_c, w_down, norm_f, *, final_norm):
    hb = TM_F // HALO
    last_hb = xall.shape[0] // HALO - 1
    lat_tiles = T_LAT // TM_F
    n_tiles = (T_ALL if with_ctx else T_LAT) // TM_F
    kern = functools.partial(_ffn_kernel, final_norm=final_norm, lat_tiles=lat_tiles)

    def chunk(j, lag):
        return jnp.clip(j - lag, 0, N_CH - 1)

    return pl.pallas_call(
        kern,
        grid=(n_tiles, N_CH + 2),
        in_specs=[
            pl.BlockSpec((TM_F, D), lambda i, j: (i, 0)),
            pl.BlockSpec((HALO, D), lambda i, j: (jnp.maximum(i * hb - 1, 0), 0)),
            pl.BlockSpec((HALO, D), lambda i, j: (jnp.minimum((i + 1) * hb, last_hb), 0)),
            pl.BlockSpec((None, 1, D), lambda i, j: (l, 0, 0)),
            _mod_spec(l, 3, TM_F),
            _mod_spec(l, 4, TM_F),
            _mod_spec(l, 5, TM_F),
            pl.BlockSpec((None, None, D, 2 * CH), lambda i, j: (l, chunk(j, 0), 0, 0)),
            pl.BlockSpec((None, None, 3, 2 * CH), lambda i, j: (l, chunk(j, 1), 0, 0)),
            pl.BlockSpec((None, CH, D), lambda i, j: (l, chunk(j, 2), 0)),
            pl.BlockSpec((1, D), lambda i, j: (0, 0)),
        ],
        out_specs=pl.BlockSpec((TM_F, D), lambda i, j: (i, 0)),
        out_shape=jax.ShapeDtypeStruct((n_tiles * TM_F, D), F32),
        scratch_shapes=([pltpu.VMEM((TM_F + 2 * HALO, D), BF16)] + 2 * [pltpu.VMEM((TM_F, CH), BF16)]
                        + 2 * [pltpu.VMEM((2 * CH // HD, TM_F + 2 * HALO, HD), F32)]),
        compiler_params=_cparams("arbitrary", "arbitrary"),
        name="ffn_final" if final_norm else "ffn",
    )(xall, xall, xall, norm2.reshape(DEPTH, 1, D), mods5, mods5, mods5,
      w_up_c, conv_f_c, w_down, norm_f.reshape(1, D))


def _rope_tables():
    t = jnp.arange(S)
    row = (t // GRID_W).astype(F32)
    col = (t % GRID_W).astype(F32)
    n_freq = HD // 4
    inv = 1.0 / (ROPE_THETA ** (jnp.arange(n_freq, dtype=F32) / n_freq))
    ar = row[:, None] * inv
    ac = col[:, None] * inv
    ang = jnp.concatenate([ar, ar, ac, ac], axis=-1)
    lane = jnp.arange(HD)
    sign = jnp.where((lane & 32) == 0, -1.0, 1.0).astype(F32)
    return jnp.cos(ang), jnp.sin(ang) * sign


def _permute_in_cols(w):
    return jnp.concatenate([
        w[..., 0:1024],
        w[..., 1536:2560],
        w[..., 2560:3584],
        w[..., 4608:7680],
        w[..., 7680:13824],
        w[..., 3584:4608],
        w[..., 1024:1536],
    ], axis=-1)


def kernel(x, c, ctx, c_ctx, w_mod, b_mod, norm1, w_in, sink, rpb, conv_c, w_pa, w_pb, w_pc, w_o,
           norm2, w_up, conv_f, w_down, norm_f):
    c8 = jnp.concatenate([c, c_ctx[None, :], jnp.zeros((8 - NB - 1, D), F32)], axis=0)
    mods = _mods(c8, w_mod, b_mod)
    mods5 = mods.reshape(DEPTH, 8, N_MOD, 1, D)
    cos, sin_signed = _rope_tables()
    tables = _bias_tables(rpb)
    w_in_b = _permute_in_cols(w_in).astype(BF16)
    w_pa_b, w_pb_b, w_pc_b = w_pa.astype(BF16), w_pb.astype(BF16), w_pc.astype(BF16)
    w_o_b, w_down_b = w_o.astype(BF16), w_down.astype(BF16)
    w_up_c, conv_f_c = _chunk_up(w_up), _chunk_major(conv_f)

    xall = jnp.concatenate([x.reshape(T_LAT, D), ctx.reshape(T_CTX, D)], axis=0)
    for l in range(DEPTH):
        last = l == DEPTH - 1
        p = _inproj(l, xall, norm1, mods5, w_in_b, cos, sin_signed)
        attn_lat = (_attn_a(p, sink[l]), _attn_b(l, p, tables))
        attn_ctx = None
        if not last:
            attn_ctx = (_ctx_attn(p, sink[l], off_q=OFF_QA, off_k=OFF_KA, off_v=OFF_VA,
                                  n_kv=A_KV, rep=A_REP, has_sink=True),
                        _ctx_attn(p, sink[l], off_q=OFF_QB, off_k=OFF_KB, off_v=OFF_VB,
                                  n_kv=B_HEADS, rep=1, has_sink=False))
        xall = _mix(l, not last, attn_lat, attn_ctx, p, xall, mods5, w_pa_b, w_pb_b, w_pc_b, w_o_b, conv_c)
        xall = _ffn(l, not last, xall, norm2, mods5, w_up_c, conv_f_c, w_down_b, norm_f, final_norm=last)
    return xall.reshape(NB, S, D)
```

```python
import functools

import numpy as np
import jax
import jax.numpy as jnp
from jax import lax
from jax.experimental import pallas as pl
from jax.experimental.pallas import tpu as pltpu

F32 = jnp.float32
BF16 = jnp.bfloat16

D = 2048
NB = 4
S = 4096
DEPTH = 4
GRID_W = 64
CTX = 256
HD = 128
A_HEADS = 8
A_KV = 2
A_REP = A_HEADS // A_KV
A_WIN = 128
A_BLOCK = 128
B_HEADS = 8
B_WIN_ROWS = 8
B_WIN_COLS = 16
C_WIDTH = 1024
D_FF = 5632
N_MOD = 6
ROPE_THETA = 10000.0
EPS = 1e-6
NEG = -1e30
SCALE = HD ** -0.5

T_LAT = NB * S
T_CTX = NB * CTX
T_ALL = T_LAT + T_CTX

OFF_QA = 0
OFF_QB = 1024
OFF_KB = 2048
OFF_UC = 3072
OFF_ZA = 6144
OFF_ZB = 8192
OFF_ZC = 10240
OFF_VB = 12288
OFF_KA = 13312
OFF_VA = 13568
IN_COLS = 13824

HALO = 8

VMEM_LIMIT = 56 * 1024 * 1024


def _cparams(*sem):
    return pltpu.CompilerParams(dimension_semantics=sem, vmem_limit_bytes=VMEM_LIMIT)


def _mod_row(i, tm):
    return jnp.minimum(i // (S // tm), NB)


def _mod_spec(l, k, tm):
    return pl.BlockSpec((None, None, None, 1, D), lambda i, j: (l, _mod_row(i, tm), k, 0, 0))


def _mods_kernel(c_ref, w_ref, b_ref, o_ref):
    cv = c_ref[...]
    sc = cv * jax.nn.sigmoid(cv)
    o_ref[...] = jnp.dot(sc, w_ref[...], preferred_element_type=F32,
                         precision=lax.Precision.HIGHEST) + b_ref[...]


def _mods(c8, w_mod, b_mod):
    tn = 1024
    ncols = N_MOD * D
    return pl.pallas_call(
        _mods_kernel,
        grid=(DEPTH, ncols // tn),
        in_specs=[
            pl.BlockSpec((8, D), lambda l, j: (0, 0)),
            pl.BlockSpec((None, D, tn), lambda l, j: (l, 0, j)),
            pl.BlockSpec((None, 1, tn), lambda l, j: (l, 0, j)),
        ],
        out_specs=pl.BlockSpec((None, 8, tn), lambda l, j: (l, 0, j)),
        out_shape=jax.ShapeDtypeStruct((DEPTH, 8, ncols), F32),
        compiler_params=_cparams("arbitrary", "arbitrary"),
        name="mods",
    )(c8, w_mod, b_mod.reshape(DEPTH, 1, ncols))


def _norm_mod(x, g, shift, scale):
    ms = jnp.mean(x * x, axis=-1, keepdims=True)
    y = x * lax.rsqrt(ms + EPS)
    y = y * g
    return y * (1.0 + scale) + shift


def _rope(u, cos, sin_signed, swap_lo):
    rot = jnp.where(swap_lo, pltpu.roll(u, 96, axis=1), pltpu.roll(u, 32, axis=1))
    return u * cos + rot * sin_signed


TM_IN = 1024
TN_IN = 1536
ROPE_CHUNKS = {
    OFF_QA // TN_IN: tuple(range((OFF_QB - OFF_QA) // HD)),
    OFF_KA // TN_IN: tuple((OFF_KA % TN_IN) // HD + c for c in range((OFF_VA - OFF_KA) // HD)),
}


def _inproj_kernel(x_ref, g_ref, shift_ref, scale_ref, w_ref, cos_ref, sin_ref, o_ref, h_ref):
    i = pl.program_id(0)
    j = pl.program_id(1)

    @pl.when(j == 0)
    def _():
        h = _norm_mod(x_ref[...], g_ref[...], shift_ref[...], scale_ref[...])
        h_ref[...] = h.astype(BF16)

    acc = jnp.dot(h_ref[...], w_ref[...], preferred_element_type=F32)
    o_ref[...] = acc.astype(BF16)

    for tile, chunks in ROPE_CHUNKS.items():
        @pl.when(jnp.logical_and(i < T_LAT // TM_IN, j == tile))
        def _():
            lane = lax.broadcasted_iota(jnp.int32, (TM_IN, HD), 1)
            swap_lo = (lane & 32) == 0
            for ch in chunks:
                cols = slice(ch * HD, (ch + 1) * HD)
                o_ref[:, cols] = _rope(acc[:, cols], cos_ref[...], sin_ref[...], swap_lo).astype(BF16)


def _inproj(l, xall, norm1, mods5, w_in, cos, sin_signed):
    tiles_per_seq = S // TM_IN
    return pl.pallas_call(
        _inproj_kernel,
        grid=(T_ALL // TM_IN, IN_COLS // TN_IN),
        in_specs=[
            pl.BlockSpec((TM_IN, D), lambda i, j: (i, 0)),
            pl.BlockSpec((None, 1, D), lambda i, j: (l, 0, 0)),
            _mod_spec(l, 0, TM_IN),
            _mod_spec(l, 1, TM_IN),
            pl.BlockSpec((None, D, TN_IN), lambda i, j: (l, 0, j)),
            pl.BlockSpec((TM_IN, HD), lambda i, j: (i % tiles_per_seq, 0)),
            pl.BlockSpec((TM_IN, HD), lambda i, j: (i % tiles_per_seq, 0)),
        ],
        out_specs=pl.BlockSpec((TM_IN, TN_IN), lambda i, j: (i, j)),
        out_shape=jax.ShapeDtypeStruct((T_ALL, IN_COLS), BF16),
        scratch_shapes=[pltpu.VMEM((TM_IN, D), BF16)],
        compiler_params=_cparams("arbitrary", "arbitrary"),
        name="inproj",
    )(xall, norm1.reshape(DEPTH, 1, D), mods5, mods5, w_in, cos, sin_signed)


A_KEYS = 3 * A_BLOCK


def _stack_heads(q, rep):
    if rep == 1:
        return q
    return jnp.concatenate([q[:, r * HD:(r + 1) * HD] for r in range(rep)], axis=0)


def _dot_nt(a, b):
    return lax.dot_general(a, b, (((1,), (1,)), ((), ())), preferred_element_type=F32)


def _attn_a_kernel(sink_ref, q_ref, k_ref, v_ref, kc_ref, vc_ref, o_ref):
    g = pl.program_id(1)
    n = pl.program_id(2)
    q4 = _stack_heads(q_ref[...], A_REP)
    start = pl.multiple_of(jnp.clip(n * A_BLOCK - A_BLOCK, 0, S - A_KEYS), A_BLOCK)
    k = k_ref[pl.ds(start, A_KEYS), :]
    v = v_ref[pl.ds(start, A_KEYS), :]
    rows = A_REP * A_BLOCK
    s_loc = _dot_nt(q4, k) * SCALE
    s_ctx = _dot_nt(q4, kc_ref[...]) * SCALE
    qpos = n * A_BLOCK + (lax.broadcasted_iota(jnp.int32, (rows, A_KEYS), 0) & (A_BLOCK - 1))
    kpos = start + lax.broadcasted_iota(jnp.int32, (rows, A_KEYS), 1)
    s_loc = jnp.where(jnp.abs(kpos - qpos) <= A_WIN, s_loc, NEG)
    ridx = lax.broadcasted_iota(jnp.int32, (rows, 1), 0)
    sink = jnp.full((rows, 1), sink_ref[g * A_REP], F32)
    for r in range(1, A_REP):
        sink = jnp.where(ridx >= r * A_BLOCK, sink_ref[g * A_REP + r], sink)
    m = jnp.maximum(jnp.maximum(jnp.max(s_loc, axis=1, keepdims=True),
                                jnp.max(s_ctx, axis=1, keepdims=True)), sink)
    e_loc = jnp.exp(s_loc - m)
    e_ctx = jnp.exp(s_ctx - m)
    denom = (jnp.sum(e_loc, axis=1, keepdims=True) + jnp.sum(e_ctx, axis=1, keepdims=True)
             + jnp.exp(sink - m))
    o = (jnp.dot(e_loc.astype(BF16), v, preferred_element_type=F32)
         + jnp.dot(e_ctx.astype(BF16), vc_ref[...], preferred_element_type=F32))
    o = o / denom
    for r in range(A_REP):
        o_ref[:, r * HD:(r + 1) * HD] = o[r * A_BLOCK:(r + 1) * A_BLOCK, :].astype(BF16)


def _attn_a(p, sink_l):
    nblk = S // A_BLOCK
    qw = A_REP * HD
    return pl.pallas_call(
        _attn_a_kernel,
        grid_spec=pltpu.PrefetchScalarGridSpec(
            num_scalar_prefetch=1,
            grid=(NB, A_KV, nblk),
            in_specs=[
                pl.BlockSpec((A_BLOCK, qw), lambda b, g, n, s: (b * nblk + n, g)),
                pl.BlockSpec((S, HD), lambda b, g, n, s: (b, OFF_KA // HD + g)),
                pl.BlockSpec((S, HD), lambda b, g, n, s: (b, OFF_VA // HD + g)),
                pl.BlockSpec((CTX, HD), lambda b, g, n, s: (T_LAT // CTX + b, OFF_KA // HD + g)),
                pl.BlockSpec((CTX, HD), lambda b, g, n, s: (T_LAT // CTX + b, OFF_VA // HD + g)),
            ],
            out_specs=pl.BlockSpec((A_BLOCK, qw), lambda b, g, n, s: (b * nblk + n, g)),
        ),
        out_shape=jax.ShapeDtypeStruct((T_LAT, A_HEADS * HD), BF16),
        compiler_params=_cparams("arbitrary", "arbitrary", "arbitrary"),
        name="attn_a",
    )(sink_l, p, p, p, p, p)


B_GROUP = 8
B_Q = B_GROUP * GRID_W
B_KROWS = 16
B_K = B_KROWS * GRID_W
B_NTAB = 2 * B_WIN_ROWS - 2


def _bias_tables(rpb):
    c = np.arange(GRID_W)[:, None]
    kc = np.arange(GRID_W)[None, :]
    cs = np.clip(c - B_WIN_COLS // 2, 0, GRID_W - B_WIN_COLS)
    ok = (kc >= cs) & (kc < cs + B_WIN_COLS)
    idx = np.clip(kc - c, 1 - B_WIN_COLS, B_WIN_COLS - 1) + B_WIN_COLS - 1
    mt = jnp.where(jnp.asarray(ok), rpb[..., idx].astype(F32), NEG)
    return jnp.concatenate([mt[:, :, :-1], mt[:, :, 1:]], axis=-1)


B_HPS = 2


def _attn_b_kernel(q_ref, k_ref, v_ref, kc_ref, vc_ref, t_ref, o_ref, s_ref, p_ref, oloc_ref, pc_ref, inv_ref):
    g = pl.program_id(2)
    n_groups = (S // GRID_W) // B_GROUP
    r0 = g * B_GROUP
    base = jnp.clip(r0 - B_WIN_ROWS // 2, 0, S // GRID_W - B_KROWS)
    kstart = base * GRID_W
    win = B_WIN_ROWS * GRID_W
    heads = [slice(hh * HD, (hh + 1) * HD) for hh in range(B_HPS)]
    sc_all = [_dot_nt(q_ref[:, hc], kc_ref[:, hc]) * SCALE for hc in heads]

    def body(off, jlo_of):
        def window(ref, i, hc):
            k0 = pl.multiple_of(kstart + jlo_of(i) * GRID_W, GRID_W)
            return ref[pl.ds(k0, win), hc]

        for hh, hc in enumerate(heads):
            for i in range(B_GROUP):
                rows = slice(i * GRID_W, (i + 1) * GRID_W)
                s_ref[hh, rows, :] = _dot_nt(q_ref[rows, hc], window(k_ref, i, hc))
        for hh, hc in enumerate(heads):
            for i in range(B_GROUP):
                rows = slice(i * GRID_W, (i + 1) * GRID_W)
                bias = jnp.concatenate(
                    [t_ref[hh, off + jlo_of(i) + 2 * pp - i + B_WIN_ROWS - 1] for pp in range(B_WIN_ROWS // 2)],
                    axis=1)
                sl = s_ref[hh, rows, :] * SCALE + bias
                sc = sc_all[hh][rows, :]
                m = jnp.maximum(jnp.max(sl, axis=1, keepdims=True), jnp.max(sc, axis=1, keepdims=True))
                e = jnp.exp(sl - m)
                ec = jnp.exp(sc - m)
                denom = jnp.sum(e, axis=1, keepdims=True) + jnp.sum(ec, axis=1, keepdims=True)
                p_ref[hh, rows, :] = e.astype(BF16)
                pc_ref[hh, rows, :] = ec.astype(BF16)
                inv_ref[hh, rows, :] = 1.0 / denom
        for hh, hc in enumerate(heads):
            for i in range(B_GROUP):
                rows = slice(i * GRID_W, (i + 1) * GRID_W)
                oloc_ref[hh, rows, :] = jnp.dot(p_ref[hh, rows, :], window(v_ref, i, hc),
                                                preferred_element_type=F32)

    half = B_WIN_ROWS // 2

    @pl.when(g == 0)
    def _():
        body(0, lambda i: max(i - half, 0))

    @pl.when(jnp.logical_and(g > 0, g < n_groups - 1))
    def _():
        body(-half, lambda i: i)

    @pl.when(g == n_groups - 1)
    def _():
        body(-(B_KROWS - B_GROUP), lambda i: min(i + half, B_KROWS - B_WIN_ROWS))

    for hh, hc in enumerate(heads):
        o = oloc_ref[hh] + jnp.dot(pc_ref[hh], vc_ref[:, hc], preferred_element_type=F32)
        o_ref[:, hc] = (o * inv_ref[hh]).astype(BF16)


def _attn_b(l, p, tables):
    n_groups = (S // GRID_W) // B_GROUP
    hw = B_HPS * HD
    return pl.pallas_call(
        _attn_b_kernel,
        grid=(B_HEADS // B_HPS, NB, n_groups),
        in_specs=[
            pl.BlockSpec((B_Q, hw), lambda h, b, g: (b * n_groups + g, OFF_QB // hw + h)),
            pl.BlockSpec((S, hw), lambda h, b, g: (b, OFF_KB // hw + h)),
            pl.BlockSpec((S, hw), lambda h, b, g: (b, OFF_VB // hw + h)),
            pl.BlockSpec((CTX, hw), lambda h, b, g: (T_LAT // CTX + b, OFF_KB // hw + h)),
            pl.BlockSpec((CTX, hw), lambda h, b, g: (T_LAT // CTX + b, OFF_VB // hw + h)),
            pl.BlockSpec((None, B_HPS, B_NTAB, GRID_W, 2 * GRID_W), lambda h, b, g: (l, h, 0, 0, 0)),
        ],
        out_specs=pl.BlockSpec((B_Q, hw), lambda h, b, g: (b * n_groups + g, h)),
        out_shape=jax.ShapeDtypeStruct((T_LAT, B_HEADS * HD), BF16),
        scratch_shapes=[
            pltpu.VMEM((B_HPS, B_Q, B_WIN_ROWS * GRID_W), F32),
            pltpu.VMEM((B_HPS, B_Q, B_WIN_ROWS * GRID_W), BF16),
            pltpu.VMEM((B_HPS, B_Q, HD), F32),
            pltpu.VMEM((B_HPS, B_Q, CTX), BF16),
            pltpu.VMEM((B_HPS, B_Q, 1), F32),
        ],
        compiler_params=_cparams("arbitrary", "arbitrary", "arbitrary"),
        name="attn_b",
    )(p, p, p, p, p, tables)


def _ctx_attn_kernel(sink_ref, q_ref, k_ref, v_ref, o_ref, *, rep, has_sink):
    g = pl.program_id(1)
    qs = _stack_heads(q_ref[...], rep)
    s = _dot_nt(qs, k_ref[...]) * SCALE
    m = jnp.max(s, axis=1, keepdims=True)
    rows = rep * CTX
    if has_sink:
        ridx = lax.broadcasted_iota(jnp.int32, (rows, 1), 0)
        sink = jnp.full((rows, 1), sink_ref[g * rep], F32)
        for r in range(1, rep):
            sink = jnp.where(ridx >= r * CTX, sink_ref[g * rep + r], sink)
        m = jnp.maximum(m, sink)
    e = jnp.exp(s - m)
    denom = jnp.sum(e, axis=1, keepdims=True)
    if has_sink:
        denom = denom + jnp.exp(sink - m)
    o = jnp.dot(e.astype(BF16), v_ref[...], preferred_element_type=F32) / denom
    for r in range(rep):
        o_ref[:, r * HD:(r + 1) * HD] = o[r * CTX:(r + 1) * CTX, :].astype(BF16)


def _ctx_attn(p, sink_l, *, off_q, off_k, off_v, n_kv, rep, has_sink):
    qw = rep * HD
    rb = T_LAT // CTX
    kern = functools.partial(_ctx_attn_kernel, rep=rep, has_sink=has_sink)
    return pl.pallas_call(
        kern,
        grid_spec=pltpu.PrefetchScalarGridSpec(
            num_scalar_prefetch=1,
            grid=(NB, n_kv),
            in_specs=[
                pl.BlockSpec((CTX, qw), lambda b, g, s: (rb + b, off_q // qw + g)),
                pl.BlockSpec((CTX, HD), lambda b, g, s: (rb + b, off_k // HD + g)),
                pl.BlockSpec((CTX, HD), lambda b, g, s: (rb + b, off_v // HD + g)),
            ],
            out_specs=pl.BlockSpec((CTX, qw), lambda b, g, s: (b, g)),
        ),
        out_shape=jax.ShapeDtypeStruct((T_CTX, n_kv * qw), BF16),
        compiler_params=_cparams("arbitrary", "arbitrary"),
        name="ctx_attn_a" if has_sink else "ctx_attn_b",
    )(sink_l, p, p, p)


TM_M = 256
Z_COLS = OFF_VB - OFF_ZA


def _mix_kernel(*refs, has_ctx):
    if has_ctx:
        a_ref, b_ref, ac_ref, bc_ref = refs[:4]
        refs = refs[4:]
    else:
        a_ref, b_ref = refs[:2]
        refs = refs[2:]
    c_ref, cp_ref, cn_ref, z_ref, x_ref, gate_ref, wa_ref, wb_ref, wc_ref, wo_ref, cw_ref, o_ref = refs
    i = pl.program_id(0)
    is_lat = i < T_LAT // TM_M
    if has_ctx:
        a = jnp.where(is_lat, a_ref[...], ac_ref[...])
        b = jnp.where(is_lat, b_ref[...], bc_ref[...])
    else:
        a, b = a_ref[...], b_ref[...]

    w = C_WIDTH

    def gated(ref):
        return ref[:, w:2 * w].astype(F32) * ref[:, 0:w].astype(F32)

    v = gated(c_ref)
    vp = gated(cp_ref)[HALO - 1:HALO, :]
    vn = gated(cn_ref)[0:1, :]
    row = lax.broadcasted_iota(jnp.int32, (TM_M, w), 0)
    v_prev = jnp.where(row == 0, vp, pltpu.roll(v, 1, axis=0))
    v_next = jnp.where(row == TM_M - 1, vn, pltpu.roll(v, TM_M - 1, axis=0))
    lmask = jnp.where(is_lat, S - 1, CTX - 1)
    pos = (i * TM_M + row) & lmask
    cw = cw_ref[...]
    conv = (cw[0:1, :] * jnp.where(pos != 0, v_prev, 0.0) + cw[1:2, :] * v
            + cw[2:3, :] * jnp.where(pos != lmask, v_next, 0.0))
    cb = (c_ref[:, 2 * w:3 * w].astype(F32) * conv).astype(BF16)

    ya = jnp.dot(a, wa_ref[...], preferred_element_type=F32)
    yb = jnp.dot(b, wb_ref[...], preferred_element_type=F32)
    yc = jnp.dot(cb, wc_ref[...], preferred_element_type=F32)
    m = (jax.nn.sigmoid(z_ref[:, 0:D].astype(F32)) * ya + jax.nn.sigmoid(z_ref[:, D:2 * D].astype(F32)) * yb
         + jax.nn.sigmoid(z_ref[:, 2 * D:3 * D].astype(F32)) * yc)
    y = jnp.dot(m.astype(BF16), wo_ref[...], preferred_element_type=F32)
    o_ref[...] = x_ref[...] + gate_ref[...] * y


def _mix(l, with_ctx, attn_lat, attn_ctx, p, xall, mods5, w_pa, w_pb, w_pc, w_o, conv_c):
    cw3 = 3 * C_WIDTH
    hb = TM_M // HALO
    last_hb = T_ALL // HALO - 1
    lat_tiles = T_LAT // TM_M
    n_tiles = (T_ALL if with_ctx else T_LAT) // TM_M
    aw, bw = A_HEADS * HD, B_HEADS * HD
    attn_specs = [pl.BlockSpec((TM_M, aw), lambda i: (jnp.minimum(i, lat_tiles - 1), 0)),
                  pl.BlockSpec((TM_M, bw), lambda i: (jnp.minimum(i, lat_tiles - 1), 0))]
    attn_args = list(attn_lat)
    if with_ctx:
        attn_specs += [pl.BlockSpec((TM_M, aw), lambda i: (jnp.maximum(i - lat_tiles, 0), 0)),
                       pl.BlockSpec((TM_M, bw), lambda i: (jnp.maximum(i - lat_tiles, 0), 0))]
        attn_args += list(attn_ctx)

    def resident(shape):
        return pl.BlockSpec((None,) + shape, lambda i: (l, 0, 0), pipeline_mode=pl.Buffered(1))

    return pl.pallas_call(
        functools.partial(_mix_kernel, has_ctx=with_ctx),
        grid=(n_tiles,),
        in_specs=attn_specs + [
            pl.BlockSpec((TM_M, cw3), lambda i: (i, OFF_UC // cw3)),
            pl.BlockSpec((HALO, cw3), lambda i: (jnp.maximum(i * hb - 1, 0), OFF_UC // cw3)),
            pl.BlockSpec((HALO, cw3), lambda i: (jnp.minimum((i + 1) * hb, last_hb), OFF_UC // cw3)),
            pl.BlockSpec((TM_M, Z_COLS), lambda i: (i, OFF_ZA // Z_COLS)),
            pl.BlockSpec((TM_M, D), lambda i: (i, 0)),
            pl.BlockSpec((None, None, None, 1, D), lambda i: (l, _mod_row(i, TM_M), 2, 0, 0)),
            resident((aw, D)),
            resident((bw, D)),
            resident((C_WIDTH, D)),
            resident((D, D)),
            pl.BlockSpec((None, 3, C_WIDTH), lambda i: (l, 0, 0)),
        ],
        out_specs=pl.BlockSpec((TM_M, D), lambda i: (i, 0)),
        out_shape=jax.ShapeDtypeStruct((n_tiles * TM_M, D), F32),
        compiler_params=_cparams("arbitrary"),
        name="mix",
    )(*attn_args, p, p, p, p, xall, mods5, w_pa, w_pb, w_pc, w_o, conv_c)


TM_F = 1024
CH = 256
N_CH = D_FF // CH
ROWS_PER_PIECE = 64


def _chunk_major(w):
    r = w.shape[1]
    w = w.reshape(DEPTH, r, 2, N_CH, CH)
    return jnp.transpose(w, (0, 3, 1, 2, 4)).reshape(DEPTH, N_CH, r, 2 * CH)


def _chunk_up_kernel(g_ref, v_ref, o_ref):
    o_ref[:, 0:CH] = g_ref[...].astype(BF16)
    o_ref[:, CH:2 * CH] = v_ref[...].astype(BF16)


def _chunk_up(w_up):
    return pl.pallas_call(
        _chunk_up_kernel,
        grid=(DEPTH, N_CH),
        in_specs=[pl.BlockSpec((None, D, CH), lambda l, k: (l, 0, k)),
                  pl.BlockSpec((None, D, CH), lambda l, k: (l, 0, N_CH + k))],
        out_specs=pl.BlockSpec((None, None, D, 2 * CH), lambda l, k: (l, k, 0, 0)),
        out_shape=jax.ShapeDtypeStruct((DEPTH, N_CH, D, 2 * CH), BF16),
        compiler_params=_cparams("arbitrary", "arbitrary"),
        name="chunk_up",
    )(w_up, w_up)


def _ffn_kernel(x_ref, xp_ref, xn_ref, g_ref, shift_ref, scale_ref, gate_ref,
                wu_ref, cw_ref, wd_ref, nf_ref, o_ref, h_ref, act0_ref, act1_ref, u0_ref, u1_ref,
                *, final_norm, lat_tiles):
    i = pl.program_id(0)
    j = pl.program_id(1)
    nj = N_CH
    n_slab = 2 * CH // HD
    half = n_slab // 2

    @pl.when(j == 0)
    def _():
        g, sh, sc = g_ref[...], shift_ref[...], scale_ref[...]
        h_ref[0:HALO, :] = _norm_mod(xp_ref[...], g, sh, sc).astype(BF16)
        h_ref[HALO:HALO + TM_F, :] = _norm_mod(x_ref[...], g, sh, sc).astype(BF16)
        h_ref[HALO + TM_F:, :] = _norm_mod(xn_ref[...], g, sh, sc).astype(BF16)
        o_ref[...] = jnp.zeros((TM_F, D), F32)

    act_ref = (act0_ref, act1_ref)
    u_ref = (u0_ref, u1_ref)

    def up(slot):
        u = jnp.dot(h_ref[...], wu_ref[...], preferred_element_type=F32)
        for c in range(n_slab):
            u_ref[slot][c] = u[:, c * HD:(c + 1) * HD]

    def epilogue(slot):
        rb = ROWS_PER_PIECE
        lmask = jnp.where(i < lat_tiles, S - 1, CTX - 1)

        def conv(c, r0, has_prev, has_next):
            u_prev = jnp.where(has_prev, u_ref[slot][c, HALO - 1 + r0:HALO - 1 + r0 + rb, :], 0.0)
            u_next = jnp.where(has_next, u_ref[slot][c, HALO + 1 + r0:HALO + 1 + r0 + rb, :], 0.0)
            cw = cw_ref[:, c * HD:(c + 1) * HD]
            return (cw[0:1, :] * u_prev + cw[1:2, :] * u_ref[slot][c, HALO + r0:HALO + r0 + rb, :]
                    + cw[2:3, :] * u_next)

        for r0 in range(0, TM_F, rb):
            pos = (i * TM_F + r0 + lax.broadcasted_iota(jnp.int32, (rb, HD), 0)) & lmask
            has_prev, has_next = pos != 0, pos != lmask
            for c in range(half):
                cgate = conv(c, r0, has_prev, has_next)
                cval = conv(half + c, r0, has_prev, has_next)
                act_ref[slot][r0:r0 + rb, c * HD:(c + 1) * HD] = (
                    cgate * jax.nn.sigmoid(cgate) * cval).astype(BF16)

    def down(slot):
        return jnp.dot(act_ref[slot][...], wd_ref[...], preferred_element_type=F32)

    @pl.when(j == 0)
    def _():
        up(0)

    @pl.when(j == 1)
    def _():
        epilogue(0)
        up(1)

    for parity in (0, 1):
        @pl.when(jnp.logical_and(jnp.logical_and(j >= 2, j < nj), j % 2 == parity))
        def _():
            epilogue(1 - parity)
            up(parity)
            o_ref[...] += down(parity)

    @pl.when(j == nj)
    def _():
        epilogue((nj - 1) % 2)
        o_ref[...] += down(nj % 2)

    @pl.when(j == nj + 1)
    def _():
        xo = x_ref[...] + gate_ref[...] * (o_ref[...] + down((nj - 1) % 2))
        if final_norm:
            ms = jnp.mean(xo * xo, axis=-1, keepdims=True)
            xo = xo * lax.rsqrt(ms + EPS) * nf_ref[...]
        o_ref[...] = xo


def _ffn(l, with_ctx, xall, norm2, mods5, w_up_c, conv_f_c, w_down, norm_f, *, final_norm):
    hb = TM_F // HALO
    last_hb = xall.shape[0] // HALO - 1
    lat_tiles = T_LAT // TM_F
    n_tiles = (T_ALL if with_ctx else T_LAT) // TM_F
    kern = functools.partial(_ffn_kernel, final_norm=final_norm, lat_tiles=lat_tiles)

    def chunk(j, lag):
        return jnp.clip(j - lag, 0, N_CH - 1)

    return pl.pallas_call(
        kern,
        grid=(n_tiles, N_CH + 2),
        in_specs=[
            pl.BlockSpec((TM_F, D), lambda i, j: (i, 0)),
            pl.BlockSpec((HALO, D), lambda i, j: (jnp.maximum(i * hb - 1, 0), 0)),
            pl.BlockSpec((HALO, D), lambda i, j: (jnp.minimum((i + 1) * hb, last_hb), 0)),
            pl.BlockSpec((None, 1, D), lambda i, j: (l, 0, 0)),
            _mod_spec(l, 3, TM_F),
            _mod_spec(l, 4, TM_F),
            _mod_spec(l, 5, TM_F),
            pl.BlockSpec((None, None, D, 2 * CH), lambda i, j: (l, chunk(j, 0), 0, 0)),
            pl.BlockSpec((None, None, 3, 2 * CH), lambda i, j: (l, chunk(j, 1), 0, 0)),
            pl.BlockSpec((None, CH, D), lambda i, j: (l, chunk(j, 2), 0)),
            pl.BlockSpec((1, D), lambda i, j: (0, 0)),
        ],
        out_specs=pl.BlockSpec((TM_F, D), lambda i, j: (i, 0)),
        out_shape=jax.ShapeDtypeStruct((n_tiles * TM_F, D), F32),
        scratch_shapes=([pltpu.VMEM((TM_F + 2 * HALO, D), BF16)] + 2 * [pltpu.VMEM((TM_F, CH), BF16)]
                        + 2 * [pltpu.VMEM((2 * CH // HD, TM_F + 2 * HALO, HD), F32)]),
        compiler_params=_cparams("arbitrary", "arbitrary"),
        name="ffn_final" if final_norm else "ffn",
    )(xall, xall, xall, norm2.reshape(DEPTH, 1, D), mods5, mods5, mods5,
      w_up_c, conv_f_c, w_down, norm_f.reshape(1, D))


def _rope_tables():
    t = jnp.arange(S)
    row = (t // GRID_W).astype(F32)
    col = (t % GRID_W).astype(F32)
    n_freq = HD // 4
    inv = 1.0 / (ROPE_THETA ** (jnp.arange(n_freq, dtype=F32) / n_freq))
    ar = row[:, None] * inv
    ac = col[:, None] * inv
    ang = jnp.concatenate([ar, ar, ac, ac], axis=-1)
    lane = jnp.arange(HD)
    sign = jnp.where((lane & 32) == 0, -1.0, 1.0).astype(F32)
    return jnp.cos(ang), jnp.sin(ang) * sign


def _permute_in_cols(w):
    return jnp.concatenate([
        w[..., 0:1024],
        w[..., 1536:2560],
        w[..., 2560:3584],
        w[..., 4608:7680],
        w[..., 7680:13824],
        w[..., 3584:4608],
        w[..., 1024:1536],
    ], axis=-1)


def kernel(x, c, ctx, c_ctx, w_mod, b_mod, norm1, w_in, sink, rpb, conv_c, w_pa, w_pb, w_pc, w_o,
           norm2, w_up, conv_f, w_down, norm_f):
    c8 = jnp.concatenate([c, c_ctx[None, :], jnp.zeros((8 - NB - 1, D), F32)], axis=0)
    mods = _mods(c8, w_mod, b_mod)
    mods5 = mods.reshape(DEPTH, 8, N_MOD, 1, D)
    cos, sin_signed = _rope_tables()
    tables = _bias_tables(rpb)
    w_in_b = _permute_in_cols(w_in).astype(BF16)
    w_pa_b, w_pb_b, w_pc_b = w_pa.astype(BF16), w_pb.astype(BF16), w_pc.astype(BF16)
    w_o_b, w_down_b = w_o.astype(BF16), w_down.astype(BF16)
    w_up_c, conv_f_c = _chunk_up(w_up), _chunk_major(conv_f)

    xall = jnp.concatenate([x.reshape(T_LAT, D), ctx.reshape(T_CTX, D)], axis=0)
    for l in range(DEPTH):
        last = l == DEPTH - 1
        p = _inproj(l, xall, norm1, mods5, w_in_b, cos, sin_signed)
        attn_lat = (_attn_a(p, sink[l]), _attn_b(l, p, tables))
        attn_ctx = None
        if not last:
            attn_ctx = (_ctx_attn(p, sink[l], off_q=OFF_QA, off_k=OFF_KA, off_v=OFF_VA,
                                  n_kv=A_KV, rep=A_REP, has_sink=True),
                        _ctx_attn(p, sink[l], off_q=OFF_QB, off_k=OFF_KB, off_v=OFF_VB,
                                  n_kv=B_HEADS, rep=1, has_sink=False))
        xall = _mix(l, not last, attn_lat, attn_ctx, p, xall, mods5, w_pa_b, w_pb_b, w_pc_b, w_o_b, conv_c)
        xall = _ffn(l, not last, xall, norm2, mods5, w_up_c, conv_f_c, w_down_b, norm_f, final_norm=last)
    return xall.reshape(NB, S, D)
```

```python
import functools

import numpy as np
import jax
import jax.numpy as jnp
from jax import lax
from jax.experimental import pallas as pl
from jax.experimental.pallas import tpu as pltpu

F32 = jnp.float32
BF16 = jnp.bfloat16

D = 2048
NB = 4
S = 4096
DEPTH = 4
GRID_W = 64
CTX = 256
HD = 128
A_HEADS = 8
A_KV = 2
A_REP = A_HEADS // A_KV
A_WIN = 128
A_BLOCK = 128
B_HEADS = 8
B_WIN_ROWS = 8
B_WIN_COLS = 16
C_WIDTH = 1024
D_FF = 5632
N_MOD = 6
ROPE_THETA = 10000.0
EPS = 1e-6
NEG = -1e30
SCALE = HD ** -0.5

T_LAT = NB * S
T_CTX = NB * CTX
T_ALL = T_LAT + T_CTX

OFF_QA = 0
OFF_QB = 1024
OFF_KB = 2048
OFF_UC = 3072
OFF_ZA = 6144
OFF_ZB = 8192
OFF_ZC = 10240
OFF_VB = 12288
OFF_KA = 13312
OFF_VA = 13568
IN_COLS = 13824

HALO = 8

VMEM_LIMIT = 56 * 1024 * 1024


def _cparams(*sem):
    return pltpu.CompilerParams(dimension_semantics=sem, vmem_limit_bytes=VMEM_LIMIT)


def _mod_row(i, tm):
    return jnp.minimum(i // (S // tm), NB)


def _mod_spec(l, k, tm):
    return pl.BlockSpec((None, None, None, 1, D), lambda i, j: (l, _mod_row(i, tm), k, 0, 0))


def _mods_kernel(c_ref, w_ref, b_ref, o_ref):
    cv = c_ref[...]
    sc = cv * jax.nn.sigmoid(cv)
    o_ref[...] = jnp.dot(sc, w_ref[...], preferred_element_type=F32,
                         precision=lax.Precision.HIGHEST) + b_ref[...]


def _mods(c8, w_mod, b_mod):
    tn = 1024
    ncols = N_MOD * D
    return pl.pallas_call(
        _mods_kernel,
        grid=(DEPTH, ncols // tn),
        in_specs=[
            pl.BlockSpec((8, D), lambda l, j: (0, 0)),
            pl.BlockSpec((None, D, tn), lambda l, j: (l, 0, j)),
            pl.BlockSpec((None, 1, tn), lambda l, j: (l, 0, j)),
        ],
        out_specs=pl.BlockSpec((None, 8, tn), lambda l, j: (l, 0, j)),
        out_shape=jax.ShapeDtypeStruct((DEPTH, 8, ncols), F32),
        compiler_params=_cparams("arbitrary", "arbitrary"),
        name="mods",
    )(c8, w_mod, b_mod.reshape(DEPTH, 1, ncols))


def _norm_mod(x, g, shift, scale):
    ms = jnp.mean(x * x, axis=-1, keepdims=True)
    y = x * lax.rsqrt(ms + EPS)
    y = y * g
    return y * (1.0 + scale) + shift


def _rope(u, cos, sin_signed, swap_lo):
    rot = jnp.where(swap_lo, pltpu.roll(u, 96, axis=1), pltpu.roll(u, 32, axis=1))
    return u * cos + rot * sin_signed


TM_IN = 1024
TN_IN = 1536
ROPE_CHUNKS = {
    OFF_QA // TN_IN: tuple(range((OFF_QB - OFF_QA) // HD)),
    OFF_KA // TN_IN: tuple((OFF_KA % TN_IN) // HD + c for c in range((OFF_VA - OFF_KA) // HD)),
}


def _inproj_kernel(x_ref, g_ref, shift_ref, scale_ref, w_ref, cos_ref, sin_ref, o_ref, h_ref):
    i = pl.program_id(0)
    j = pl.program_id(1)

    @pl.when(j == 0)
    def _():
        h = _norm_mod(x_ref[...], g_ref[...], shift_ref[...], scale_ref[...])
        h_ref[...] = h.astype(BF16)

    acc = jnp.dot(h_ref[...], w_ref[...], preferred_element_type=F32)
    o_ref[...] = acc.astype(BF16)

    for tile, chunks in ROPE_CHUNKS.items():
        @pl.when(jnp.logical_and(i < T_LAT // TM_IN, j == tile))
        def _():
            lane = lax.broadcasted_iota(jnp.int32, (TM_IN, HD), 1)
            swap_lo = (lane & 32) == 0
            for ch in chunks:
                cols = slice(ch * HD, (ch + 1) * HD)
                o_ref[:, cols] = _rope(acc[:, cols], cos_ref[...], sin_ref[...], swap_lo).astype(BF16)


def _inproj(l, xall, norm1, mods5, w_in, cos, sin_signed):
    tiles_per_seq = S // TM_IN
    return pl.pallas_call(
        _inproj_kernel,
        grid=(T_ALL // TM_IN, IN_COLS // TN_IN),
        in_specs=[
            pl.BlockSpec((TM_IN, D), lambda i, j: (i, 0)),
            pl.BlockSpec((None, 1, D), lambda i, j: (l, 0, 0)),
            _mod_spec(l, 0, TM_IN),
            _mod_spec(l, 1, TM_IN),
            pl.BlockSpec((None, D, TN_IN), lambda i, j: (l, 0, j)),
            pl.BlockSpec((TM_IN, HD), lambda i, j: (i % tiles_per_seq, 0)),
            pl.BlockSpec((TM_IN, HD), lambda i, j: (i % tiles_per_seq, 0)),
        ],
        out_specs=pl.BlockSpec((TM_IN, TN_IN), lambda i, j: (i, j)),
        out_shape=jax.ShapeDtypeStruct((T_ALL, IN_COLS), BF16),
        scratch_shapes=[pltpu.VMEM((TM_IN, D), BF16)],
        compiler_params=_cparams("arbitrary", "arbitrary"),
        name="inproj",
    )(xall, norm1.reshape(DEPTH, 1, D), mods5, mods5, w_in, cos, sin_signed)


A_KEYS = 3 * A_BLOCK


def _stack_heads(q, rep):
    if rep == 1:
        return q
    return jnp.concatenate([q[:, r * HD:(r + 1) * HD] for r in range(rep)], axis=0)


def _dot_nt(a, b):
    return lax.dot_general(a, b, (((1,), (1,)), ((), ())), preferred_element_type=F32)


def _attn_a_kernel(sink_ref, q_ref, k_ref, v_ref, kc_ref, vc_ref, o_ref):
    n = pl.program_id(1)
    start = pl.multiple_of(jnp.clip(n * A_BLOCK - A_BLOCK, 0, S - A_KEYS), A_BLOCK)
    rows = A_REP * A_BLOCK
    qw = A_REP * HD
    qpos = n * A_BLOCK + (lax.broadcasted_iota(jnp.int32, (rows, A_KEYS), 0) & (A_BLOCK - 1))
    kpos = start + lax.broadcasted_iota(jnp.int32, (rows, A_KEYS), 1)
    in_window = jnp.abs(kpos - qpos) <= A_WIN
    ridx = lax.broadcasted_iota(jnp.int32, (rows, 1), 0)
    groups = range(A_KV)
    kv = [slice(g * HD, (g + 1) * HD) for g in groups]

    q4 = [_stack_heads(q_ref[:, g * qw:(g + 1) * qw], A_REP) for g in groups]
    s_loc = [jnp.where(in_window, _dot_nt(q4[g], k_ref[pl.ds(start, A_KEYS), kv[g]]) * SCALE, NEG)
             for g in groups]
    s_ctx = [_dot_nt(q4[g], kc_ref[:, kv[g]]) * SCALE for g in groups]
    e_loc, e_ctx, denom = [], [], []
    for g in groups:
        sink = jnp.full((rows, 1), sink_ref[g * A_REP], F32)
        for r in range(1, A_REP):
            sink = jnp.where(ridx >= r * A_BLOCK, sink_ref[g * A_REP + r], sink)
        m = jnp.maximum(jnp.maximum(jnp.max(s_loc[g], axis=1, keepdims=True),
                                    jnp.max(s_ctx[g], axis=1, keepdims=True)), sink)
        e_loc.append(jnp.exp(s_loc[g] - m))
        e_ctx.append(jnp.exp(s_ctx[g] - m))
        denom.append(jnp.sum(e_loc[g], axis=1, keepdims=True) + jnp.sum(e_ctx[g], axis=1, keepdims=True)
                     + jnp.exp(sink - m))
    for g in groups:
        o = (jnp.dot(e_loc[g].astype(BF16), v_ref[pl.ds(start, A_KEYS), kv[g]], preferred_element_type=F32)
             + jnp.dot(e_ctx[g].astype(BF16), vc_ref[:, kv[g]], preferred_element_type=F32))
        o = o / denom[g]
        for r in range(A_REP):
            h = g * A_REP + r
            o_ref[:, h * HD:(h + 1) * HD] = o[r * A_BLOCK:(r + 1) * A_BLOCK, :].astype(BF16)


def _attn_a(p, sink_l):
    nblk = S // A_BLOCK
    aw = A_HEADS * HD
    kvw = A_KV * HD
    return pl.pallas_call(
        _attn_a_kernel,
        grid_spec=pltpu.PrefetchScalarGridSpec(
            num_scalar_prefetch=1,
            grid=(NB, nblk),
            in_specs=[
                pl.BlockSpec((A_BLOCK, aw), lambda b, n, s: (b * nblk + n, OFF_QA // aw)),
                pl.BlockSpec((S, kvw), lambda b, n, s: (b, OFF_KA // kvw)),
                pl.BlockSpec((S, kvw), lambda b, n, s: (b, OFF_VA // kvw)),
                pl.BlockSpec((CTX, kvw), lambda b, n, s: (T_LAT // CTX + b, OFF_KA // kvw)),
                pl.BlockSpec((CTX, kvw), lambda b, n, s: (T_LAT // CTX + b, OFF_VA // kvw)),
            ],
            out_specs=pl.BlockSpec((A_BLOCK, aw), lambda b, n, s: (b * nblk + n, 0)),
        ),
        out_shape=jax.ShapeDtypeStruct((T_LAT, aw), BF16),
        compiler_params=_cparams("arbitrary", "arbitrary"),
        name="attn_a",
    )(sink_l, p, p, p, p, p)


B_GROUP = 8
B_Q = B_GROUP * GRID_W
B_KROWS = 16
B_K = B_KROWS * GRID_W
B_NTAB = 2 * B_WIN_ROWS - 2


def _bias_tables(rpb):
    c = np.arange(GRID_W)[:, None]
    kc = np.arange(GRID_W)[None, :]
    cs = np.clip(c - B_WIN_COLS // 2, 0, GRID_W - B_WIN_COLS)
    ok = (kc >= cs) & (kc < cs + B_WIN_COLS)
    idx = np.clip(kc - c, 1 - B_WIN_COLS, B_WIN_COLS - 1) + B_WIN_COLS - 1
    mt = jnp.where(jnp.asarray(ok), rpb[..., idx].astype(F32), NEG)
    return jnp.concatenate([mt[:, :, :-1], mt[:, :, 1:]], axis=-1)


B_HPS = 2


def _attn_b_kernel(q_ref, k_ref, v_ref, kc_ref, vc_ref, t_ref, o_ref, s_ref, p_ref, oloc_ref, pc_ref, inv_ref):
    g = pl.program_id(2)
    n_groups = (S // GRID_W) // B_GROUP
    r0 = g * B_GROUP
    base = jnp.clip(r0 - B_WIN_ROWS // 2, 0, S // GRID_W - B_KROWS)
    kstart = base * GRID_W
    win = B_WIN_ROWS * GRID_W
    heads = [slice(hh * HD, (hh + 1) * HD) for hh in range(B_HPS)]
    sc_all = [_dot_nt(q_ref[:, hc], kc_ref[:, hc]) * SCALE for hc in heads]

    def body(off, jlo_of):
        def window(ref, i, hc):
            k0 = pl.multiple_of(kstart + jlo_of(i) * GRID_W, GRID_W)
            return ref[pl.ds(k0, win), hc]

        for hh, hc in enumerate(heads):
            for i in range(B_GROUP):
                rows = slice(i * GRID_W, (i + 1) * GRID_W)
                s_ref[hh, rows, :] = _dot_nt(q_ref[rows, hc], window(k_ref, i, hc))
        for hh, hc in enumerate(heads):
            for i in range(B_GROUP):
                rows = slice(i * GRID_W, (i + 1) * GRID_W)
                bias = jnp.concatenate(
                    [t_ref[hh, off + jlo_of(i) + 2 * pp - i + B_WIN_ROWS - 1] for pp in range(B_WIN_ROWS // 2)],
                    axis=1)
                sl = s_ref[hh, rows, :] * SCALE + bias
                sc = sc_all[hh][rows, :]
                m = jnp.maximum(jnp.max(sl, axis=1, keepdims=True), jnp.max(sc, axis=1, keepdims=True))
                e = jnp.exp(sl - m)
                ec = jnp.exp(sc - m)
                denom = jnp.sum(e, axis=1, keepdims=True) + jnp.sum(ec, axis=1, keepdims=True)
                p_ref[hh, rows, :] = e.astype(BF16)
                pc_ref[hh, rows, :] = ec.astype(BF16)
                inv_ref[hh, rows, :] = 1.0 / denom
        for hh, hc in enumerate(heads):
            for i in range(B_GROUP):
                rows = slice(i * GRID_W, (i + 1) * GRID_W)
                oloc_ref[hh, rows, :] = jnp.dot(p_ref[hh, rows, :], window(v_ref, i, hc),
                                                preferred_element_type=F32)

    half = B_WIN_ROWS // 2

    @pl.when(g == 0)
    def _():
        body(0, lambda i: max(i - half, 0))

    @pl.when(jnp.logical_and(g > 0, g < n_groups - 1))
    def _():
        body(-half, lambda i: i)

    @pl.when(g == n_groups - 1)
    def _():
        body(-(B_KROWS - B_GROUP), lambda i: min(i + half, B_KROWS - B_WIN_ROWS))

    for hh, hc in enumerate(heads):
        o = oloc_ref[hh] + jnp.dot(pc_ref[hh], vc_ref[:, hc], preferred_element_type=F32)
        o_ref[:, hc] = (o * inv_ref[hh]).astype(BF16)


def _attn_b(l, p, tables):
    n_groups = (S // GRID_W) // B_GROUP
    hw = B_HPS * HD
    return pl.pallas_call(
        _attn_b_kernel,
        grid=(B_HEADS // B_HPS, NB, n_groups),
        in_specs=[
            pl.BlockSpec((B_Q, hw), lambda h, b, g: (b * n_groups + g, OFF_QB // hw + h)),
            pl.BlockSpec((S, hw), lambda h, b, g: (b, OFF_KB // hw + h)),
            pl.BlockSpec((S, hw), lambda h, b, g: (b, OFF_VB // hw + h)),
            pl.BlockSpec((CTX, hw), lambda h, b, g: (T_LAT // CTX + b, OFF_KB // hw + h)),
            pl.BlockSpec((CTX, hw), lambda h, b, g: (T_LAT // CTX + b, OFF_VB // hw + h)),
            pl.BlockSpec((None, B_HPS, B_NTAB, GRID_W, 2 * GRID_W), lambda h, b, g: (l, h, 0, 0, 0)),
        ],
        out_specs=pl.BlockSpec((B_Q, hw), lambda h, b, g: (b * n_groups + g, h)),
        out_shape=jax.ShapeDtypeStruct((T_LAT, B_HEADS * HD), BF16),
        scratch_shapes=[
            pltpu.VMEM((B_HPS, B_Q, B_WIN_ROWS * GRID_W), F32),
            pltpu.VMEM((B_HPS, B_Q, B_WIN_ROWS * GRID_W), BF16),
            pltpu.VMEM((B_HPS, B_Q, HD), F32),
            pltpu.VMEM((B_HPS, B_Q, CTX), BF16),
            pltpu.VMEM((B_HPS, B_Q, 1), F32),
        ],
        compiler_params=_cparams("arbitrary", "arbitrary", "arbitrary"),
        name="attn_b",
    )(p, p, p, p, p, tables)


def _ctx_attn_kernel(sink_ref, q_ref, k_ref, v_ref, o_ref, *, rep, has_sink):
    g = pl.program_id(1)
    qs = _stack_heads(q_ref[...], rep)
    s = _dot_nt(qs, k_ref[...]) * SCALE
    m = jnp.max(s, axis=1, keepdims=True)
    rows = rep * CTX
    if has_sink:
        ridx = lax.broadcasted_iota(jnp.int32, (rows, 1), 0)
        sink = jnp.full((rows, 1), sink_ref[g * rep], F32)
        for r in range(1, rep):
            sink = jnp.where(ridx >= r * CTX, sink_ref[g * rep + r], sink)
        m = jnp.maximum(m, sink)
    e = jnp.exp(s - m)
    denom = jnp.sum(e, axis=1, keepdims=True)
    if has_sink:
        denom = denom + jnp.exp(sink - m)
    o = jnp.dot(e.astype(BF16), v_ref[...], preferred_element_type=F32) / denom
    for r in range(rep):
        o_ref[:, r * HD:(r + 1) * HD] = o[r * CTX:(r + 1) * CTX, :].astype(BF16)


def _ctx_attn(p, sink_l, *, off_q, off_k, off_v, n_kv, rep, has_sink):
    qw = rep * HD
    rb = T_LAT // CTX
    kern = functools.partial(_ctx_attn_kernel, rep=rep, has_sink=has_sink)
    return pl.pallas_call(
        kern,
        grid_spec=pltpu.PrefetchScalarGridSpec(
            num_scalar_prefetch=1,
            grid=(NB, n_kv),
            in_specs=[
                pl.BlockSpec((CTX, qw), lambda b, g, s: (rb + b, off_q // qw + g)),
                pl.BlockSpec((CTX, HD), lambda b, g, s: (rb + b, off_k // HD + g)),
                pl.BlockSpec((CTX, HD), lambda b, g, s: (rb + b, off_v // HD + g)),
            ],
            out_specs=pl.BlockSpec((CTX, qw), lambda b, g, s: (b, g)),
        ),
        out_shape=jax.ShapeDtypeStruct((T_CTX, n_kv * qw), BF16),
        compiler_params=_cparams("arbitrary", "arbitrary"),
        name="ctx_attn_a" if has_sink else "ctx_attn_b",
    )(sink_l, p, p, p)


TM_M = 256
Z_COLS = OFF_VB - OFF_ZA


def _mix_kernel(*refs, has_ctx):
    if has_ctx:
        a_ref, b_ref, ac_ref, bc_ref = refs[:4]
        refs = refs[4:]
    else:
        a_ref, b_ref = refs[:2]
        refs = refs[2:]
    c_ref, cp_ref, cn_ref, z_ref, x_ref, gate_ref, wa_ref, wb_ref, wc_ref, wo_ref, cw_ref, o_ref = refs
    i = pl.program_id(0)
    is_lat = i < T_LAT // TM_M
    if has_ctx:
        a = jnp.where(is_lat, a_ref[...], ac_ref[...])
        b = jnp.where(is_lat, b_ref[...], bc_ref[...])
    else:
        a, b = a_ref[...], b_ref[...]

    w = C_WIDTH

    def gated(ref):
        return ref[:, w:2 * w].astype(F32) * ref[:, 0:w].astype(F32)

    v = gated(c_ref)
    vp = gated(cp_ref)[HALO - 1:HALO, :]
    vn = gated(cn_ref)[0:1, :]
    row = lax.broadcasted_iota(jnp.int32, (TM_M, w), 0)
    v_prev = jnp.where(row == 0, vp, pltpu.roll(v, 1, axis=0))
    v_next = jnp.where(row == TM_M - 1, vn, pltpu.roll(v, TM_M - 1, axis=0))
    lmask = jnp.where(is_lat, S - 1, CTX - 1)
    pos = (i * TM_M + row) & lmask
    cw = cw_ref[...]
    conv = (cw[0:1, :] * jnp.where(pos != 0, v_prev, 0.0) + cw[1:2, :] * v
            + cw[2:3, :] * jnp.where(pos != lmask, v_next, 0.0))
    cb = (c_ref[:, 2 * w:3 * w].astype(F32) * conv).astype(BF16)

    ya = jnp.dot(a, wa_ref[...], preferred_element_type=F32)
    yb = jnp.dot(b, wb_ref[...], preferred_element_type=F32)
    yc = jnp.dot(cb, wc_ref[...], preferred_element_type=F32)
    m = (jax.nn.sigmoid(z_ref[:, 0:D].astype(F32)) * ya + jax.nn.sigmoid(z_ref[:, D:2 * D].astype(F32)) * yb
         + jax.nn.sigmoid(z_ref[:, 2 * D:3 * D].astype(F32)) * yc)
    y = jnp.dot(m.astype(BF16), wo_ref[...], preferred_element_type=F32)
    o_ref[...] = x_ref[...] + gate_ref[...] * y


def _mix(l, with_ctx, attn_lat, attn_ctx, p, xall, mods5, w_pa, w_pb, w_pc, w_o, conv_c):
    cw3 = 3 * C_WIDTH
    hb = TM_M // HALO
    last_hb = T_ALL // HALO - 1
    lat_tiles = T_LAT // TM_M
    n_tiles = (T_ALL if with_ctx else T_LAT) // TM_M
    aw, bw = A_HEADS * HD, B_HEADS * HD
    attn_specs = [pl.BlockSpec((TM_M, aw), lambda i: (jnp.minimum(i, lat_tiles - 1), 0)),
                  pl.BlockSpec((TM_M, bw), lambda i: (jnp.minimum(i, lat_tiles - 1), 0))]
    attn_args = list(attn_lat)
    if with_ctx:
        attn_specs += [pl.BlockSpec((TM_M, aw), lambda i: (jnp.maximum(i - lat_tiles, 0), 0)),
                       pl.BlockSpec((TM_M, bw), lambda i: (jnp.maximum(i - lat_tiles, 0), 0))]
        attn_args += list(attn_ctx)

    def resident(shape):
        return pl.BlockSpec((None,) + shape, lambda i: (l, 0, 0), pipeline_mode=pl.Buffered(1))

    return pl.pallas_call(
        functools.partial(_mix_kernel, has_ctx=with_ctx),
        grid=(n_tiles,),
        in_specs=attn_specs + [
            pl.BlockSpec((TM_M, cw3), lambda i: (i, OFF_UC // cw3)),
            pl.BlockSpec((HALO, cw3), lambda i: (jnp.maximum(i * hb - 1, 0), OFF_UC // cw3)),
            pl.BlockSpec((HALO, cw3), lambda i: (jnp.minimum((i + 1) * hb, last_hb), OFF_UC // cw3)),
            pl.BlockSpec((TM_M, Z_COLS), lambda i: (i, OFF_ZA // Z_COLS)),
            pl.BlockSpec((TM_M, D), lambda i: (i, 0)),
            pl.BlockSpec((None, None, None, 1, D), lambda i: (l, _mod_row(i, TM_M), 2, 0, 0)),
            resident((aw, D)),
            resident((bw, D)),
            resident((C_WIDTH, D)),
            resident((D, D)),
            pl.BlockSpec((None, 3, C_WIDTH), lambda i: (l, 0, 0)),
        ],
        out_specs=pl.BlockSpec((TM_M, D), lambda i: (i, 0)),
        out_shape=jax.ShapeDtypeStruct((n_tiles * TM_M, D), F32),
        compiler_params=_cparams("arbitrary"),
        name="mix",
    )(*attn_args, p, p, p, p, xall, mods5, w_pa, w_pb, w_pc, w_o, conv_c)


TM_F = 1024
CH = 256
N_CH = D_FF // CH
ROWS_PER_PIECE = 64


def _chunk_major(w):
    r = w.shape[1]
    w = w.reshape(DEPTH, r, 2, N_CH, CH)
    return jnp.transpose(w, (0, 3, 1, 2, 4)).reshape(DEPTH, N_CH, r, 2 * CH)


def _chunk_up_kernel(g_ref, v_ref, o_ref):
    o_ref[:, 0:CH] = g_ref[...].astype(BF16)
    o_ref[:, CH:2 * CH] = v_ref[...].astype(BF16)


def _chunk_up(w_up):
    return pl.pallas_call(
        _chunk_up_kernel,
        grid=(DEPTH, N_CH),
        in_specs=[pl.BlockSpec((None, D, CH), lambda l, k: (l, 0, k)),
                  pl.BlockSpec((None, D, CH), lambda l, k: (l, 0, N_CH + k))],
        out_specs=pl.BlockSpec((None, None, D, 2 * CH), lambda l, k: (l, k, 0, 0)),
        out_shape=jax.ShapeDtypeStruct((DEPTH, N_CH, D, 2 * CH), BF16),
        compiler_params=_cparams("arbitrary", "arbitrary"),
        name="chunk_up",
    )(w_up, w_up)


def _ffn_kernel(x_ref, xp_ref, xn_ref, g_ref, shift_ref, scale_ref, gate_ref,
                wu_ref, cw_ref, wd_ref, nf_ref, o_ref, h_ref, act0_ref, act1_ref, u0_ref, u1_ref,
                *, final_norm, lat_tiles):
    i = pl.program_id(0)
    j = pl.program_id(1)
    nj = N_CH
    n_slab = 2 * CH // HD
    half = n_slab // 2

    @pl.when(j == 0)
    def _():
        g, sh, sc = g_ref[...], shift_ref[...], scale_ref[...]
        h_ref[0:HALO, :] = _norm_mod(xp_ref[...], g, sh, sc).astype(BF16)
        h_ref[HALO:HALO + TM_F, :] = _norm_mod(x_ref[...], g, sh, sc).astype(BF16)
        h_ref[HALO + TM_F:, :] = _norm_mod(xn_ref[...], g, sh, sc).astype(BF16)
        o_ref[...] = jnp.zeros((TM_F, D), F32)

    act_ref = (act0_ref, act1_ref)
    u_ref = (u0_ref, u1_ref)

    def up(slot):
        u = jnp.dot(h_ref[...], wu_ref[...], preferred_element_type=F32)
        for c in range(n_slab):
            u_ref[slot][c] = u[:, c * HD:(c + 1) * HD]

    def epilogue(slot):
        rb = ROWS_PER_PIECE
        lmask = jnp.where(i < lat_tiles, S - 1, CTX - 1)

        def conv(c, r0, has_prev, has_next):
            u_prev = jnp.where(has_prev, u_ref[slot][c, HALO - 1 + r0:HALO - 1 + r0 + rb, :], 0.0)
            u_next = jnp.where(has_next, u_ref[slot][c, HALO + 1 + r0:HALO + 1 + r0 + rb, :], 0.0)
            cw = cw_ref[:, c * HD:(c + 1) * HD]
            return (cw[0:1, :] * u_prev + cw[1:2, :] * u_ref[slot][c, HALO + r0:HALO + r0 + rb, :]
                    + cw[2:3, :] * u_next)

        for r0 in range(0, TM_F, rb):
            pos = (i * TM_F + r0 + lax.broadcasted_iota(jnp.int32, (rb, HD), 0)) & lmask
            has_prev, has_next = pos != 0, pos != lmask
            for c in range(half):
                cgate = conv(c, r0, has_prev, has_next)
                cval = conv(half + c, r0, has_prev, has_next)
                act_ref[slot][r0:r0 + rb, c * HD:(c + 1) * HD] = (
                    cgate * jax.nn.sigmoid(cgate) * cval).astype(BF16)

    def down(slot):
        return jnp.dot(act_ref[slot][...], wd_ref[...], preferred_element_type=F32)

    @pl.when(j == 0)
    def _():
        up(0)

    @pl.when(j == 1)
    def _():
        epilogue(0)
        up(1)

    for parity in (0, 1):
        @pl.when(jnp.logical_and(jnp.logical_and(j >= 2, j < nj), j % 2 == parity))
        def _():
            epilogue(1 - parity)
            up(parity)
            o_ref[...] += down(parity)

    @pl.when(j == nj)
    def _():
        epilogue((nj - 1) % 2)
        o_ref[...] += down(nj % 2)

    @pl.when(j == nj + 1)
    def _():
        xo = x_ref[...] + gate_ref[...] * (o_ref[...] + down((nj - 1) % 2))
        if final_norm:
            ms = jnp.mean(xo * xo, axis=-1, keepdims=True)
            xo = xo * lax.rsqrt(ms + EPS) * nf_ref[...]
        o_ref[...] = xo


def _ffn(l, with_ctx, xall, norm2, mods5, w_up_c, conv_f_c, w_down, norm_f, *, final_norm):
    hb = TM_F // HALO
    last_hb = xall.shape[0] // HALO - 1
    lat_tiles = T_LAT // TM_F
    n_tiles = (T_ALL if with_ctx else T_LAT) // TM_F
    kern = functools.partial(_ffn_kernel, final_norm=final_norm, lat_tiles=lat_tiles)

    def chunk(j, lag):
        return jnp.clip(j - lag, 0, N_CH - 1)

    return pl.pallas_call(
        kern,
        grid=(n_tiles, N_CH + 2),
        in_specs=[
            pl.BlockSpec((TM_F, D), lambda i, j: (i, 0)),
            pl.BlockSpec((HALO, D), lambda i, j: (jnp.maximum(i * hb - 1, 0), 0)),
            pl.BlockSpec((HALO, D), lambda i, j: (jnp.minimum((i + 1) * hb, last_hb), 0)),
            pl.BlockSpec((None, 1, D), lambda i, j: (l, 0, 0)),
            _mod_spec(l, 3, TM_F),
            _mod_spec(l, 4, TM_F),
            _mod_spec(l, 5, TM_F),
            pl.BlockSpec((None, None, D, 2 * CH), lambda i, j: (l, chunk(j, 0), 0, 0)),
            pl.BlockSpec((None, None, 3, 2 * CH), lambda i, j: (l, chunk(j, 1), 0, 0)),
            pl.BlockSpec((None, CH, D), lambda i, j: (l, chunk(j, 2), 0)),
            pl.BlockSpec((1, D), lambda i, j: (0, 0)),
        ],
        out_specs=pl.BlockSpec((TM_F, D), lambda i, j: (i, 0)),
        out_shape=jax.ShapeDtypeStruct((n_tiles * TM_F, D), F32),
        scratch_shapes=([pltpu.VMEM((TM_F + 2 * HALO, D), BF16)] + 2 * [pltpu.VMEM((TM_F, CH), BF16)]
                        + 2 * [pltpu.VMEM((2 * CH // HD, TM_F + 2 * HALO, HD), F32)]),
        compiler_params=_cparams("arbitrary", "arbitrary"),
        name="ffn_final" if final_norm else "ffn",
    )(xall, xall, xall, norm2.reshape(DEPTH, 1, D), mods5, mods5, mods5,
      w_up_c, conv_f_c, w_down, norm_f.reshape(1, D))


def _rope_tables():
    t = jnp.arange(S)
    row = (t // GRID_W).astype(F32)
    col = (t % GRID_W).astype(F32)
    n_freq = HD // 4
    inv = 1.0 / (ROPE_THETA ** (jnp.arange(n_freq, dtype=F32) / n_freq))
    ar = row[:, None] * inv
    ac = col[:, None] * inv
    ang = jnp.concatenate([ar, ar, ac, ac], axis=-1)
    lane = jnp.arange(HD)
    sign = jnp.where((lane & 32) == 0, -1.0, 1.0).astype(F32)
    return jnp.cos(ang), jnp.sin(ang) * sign


def _permute_in_cols(w):
    return jnp.concatenate([
        w[..., 0:1024],
        w[..., 1536:2560],
        w[..., 2560:3584],
        w[..., 4608:7680],
        w[..., 7680:13824],
        w[..., 3584:4608],
        w[..., 1024:1536],
    ], axis=-1)


def kernel(x, c, ctx, c_ctx, w_mod, b_mod, norm1, w_in, sink, rpb, conv_c, w_pa, w_pb, w_pc, w_o,
           norm2, w_up, conv_f, w_down, norm_f):
    c8 = jnp.concatenate([c, c_ctx[None, :], jnp.zeros((8 - NB - 1, D), F32)], axis=0)
    mods = _mods(c8, w_mod, b_mod)
    mods5 = mods.reshape(DEPTH, 8, N_MOD, 1, D)
    cos, sin_signed = _rope_tables()
    tables = _bias_tables(rpb)
    w_in_b = _permute_in_cols(w_in).astype(BF16)
    w_pa_b, w_pb_b, w_pc_b = w_pa.astype(BF16), w_pb.astype(BF16), w_pc.astype(BF16)
    w_o_b, w_down_b = w_o.astype(BF16), w_down.astype(BF16)
    w_up_c, conv_f_c = _chunk_up(w_up), _chunk_major(conv_f)

    xall = jnp.concatenate([x.reshape(T_LAT, D), ctx.reshape(T_CTX, D)], axis=0)
    for l in range(DEPTH):
        last = l == DEPTH - 1
        p = _inproj(l, xall, norm1, mods5, w_in_b, cos, sin_signed)
        attn_lat = (_attn_a(p, sink[l]), _attn_b(l, p, tables))
        attn_ctx = None
        if not last:
            attn_ctx = (_ctx_attn(p, sink[l], off_q=OFF_QA, off_k=OFF_KA, off_v=OFF_VA,
                                  n_kv=A_KV, rep=A_REP, has_sink=True),
                        _ctx_attn(p, sink[l], off_q=OFF_QB, off_k=OFF_KB, off_v=OFF_VB,
                                  n_kv=B_HEADS, rep=1, has_sink=False))
        xall = _mix(l, not last, attn_lat, attn_ctx, p, xall, mods5, w_pa_b, w_pb_b, w_pc_b, w_o_b, conv_c)
        xall = _ffn(l, not last, xall, norm2, mods5, w_up_c, conv_f_c, w_down_b, norm_f, final_norm=last)
    return xall.reshape(NB, S, D)
```

```python
import functools

import numpy as np
import jax
import jax.numpy as jnp
from jax import lax
from jax.experimental import pallas as pl
from jax.experimental.pallas import tpu as pltpu

F32 = jnp.float32
BF16 = jnp.bfloat16

D = 2048
NB = 4
S = 4096
DEPTH = 4
GRID_W = 64
CTX = 256
HD = 128
A_HEADS = 8
A_KV = 2
A_REP = A_HEADS // A_KV
A_WIN = 128
A_BLOCK = 128
B_HEADS = 8
B_WIN_ROWS = 8
B_WIN_COLS = 16
C_WIDTH = 1024
D_FF = 5632
N_MOD = 6
ROPE_THETA = 10000.0
EPS = 1e-6
NEG = -1e30
SCALE = HD ** -0.5

T_LAT = NB * S
T_CTX = NB * CTX
T_ALL = T_LAT + T_CTX

OFF_QA = 0
OFF_QB = 1024
OFF_KB = 2048
OFF_UC = 3072
OFF_ZA = 6144
OFF_ZB = 8192
OFF_ZC = 10240
OFF_VB = 12288
OFF_KA = 13312
OFF_VA = 13568
IN_COLS = 13824

HALO = 8

VMEM_LIMIT = 56 * 1024 * 1024


def _cparams(*sem):
    return pltpu.CompilerParams(dimension_semantics=sem, vmem_limit_bytes=VMEM_LIMIT)


def _mod_row(i, tm):
    return jnp.minimum(i // (S // tm), NB)


def _mod_spec(l, k, tm):
    return pl.BlockSpec((None, None, None, 1, D), lambda i, j: (l, _mod_row(i, tm), k, 0, 0))


def _mods_kernel(c_ref, w_ref, b_ref, o_ref):
    cv = c_ref[...]
    sc = cv * jax.nn.sigmoid(cv)
    o_ref[...] = jnp.dot(sc, w_ref[...], preferred_element_type=F32,
                         precision=lax.Precision.HIGHEST) + b_ref[...]


def _mods(c8, w_mod, b_mod):
    tn = 1024
    ncols = N_MOD * D
    return pl.pallas_call(
        _mods_kernel,
        grid=(DEPTH, ncols // tn),
        in_specs=[
            pl.BlockSpec((8, D), lambda l, j: (0, 0)),
            pl.BlockSpec((None, D, tn), lambda l, j: (l, 0, j)),
            pl.BlockSpec((None, 1, tn), lambda l, j: (l, 0, j)),
        ],
        out_specs=pl.BlockSpec((None, 8, tn), lambda l, j: (l, 0, j)),
        out_shape=jax.ShapeDtypeStruct((DEPTH, 8, ncols), F32),
        compiler_params=_cparams("arbitrary", "arbitrary"),
        name="mods",
    )(c8, w_mod, b_mod.reshape(DEPTH, 1, ncols))


def _norm_mod(x, g, shift, scale):
    ms = jnp.mean(x * x, axis=-1, keepdims=True)
    y = x * lax.rsqrt(ms + EPS)
    y = y * g
    return y * (1.0 + scale) + shift


def _rope(u, cos, sin_signed, swap_lo):
    rot = jnp.where(swap_lo, pltpu.roll(u, 96, axis=1), pltpu.roll(u, 32, axis=1))
    return u * cos + rot * sin_signed


TM_IN = 1024
TN_IN = 1536
ROPE_CHUNKS = {
    OFF_QA // TN_IN: tuple(range((OFF_QB - OFF_QA) // HD)),
    OFF_KA // TN_IN: tuple((OFF_KA % TN_IN) // HD + c for c in range((OFF_VA - OFF_KA) // HD)),
}


def _inproj_kernel(x_ref, g_ref, shift_ref, scale_ref, w_ref, cos_ref, sin_ref, o_ref, h_ref):
    i = pl.program_id(0)
    j = pl.program_id(1)

    @pl.when(j == 0)
    def _():
        h = _norm_mod(x_ref[...], g_ref[...], shift_ref[...], scale_ref[...])
        h_ref[...] = h.astype(BF16)

    acc = jnp.dot(h_ref[...], w_ref[...], preferred_element_type=F32)
    o_ref[...] = acc.astype(BF16)

    for tile, chunks in ROPE_CHUNKS.items():
        @pl.when(jnp.logical_and(i < T_LAT // TM_IN, j == tile))
        def _():
            lane = lax.broadcasted_iota(jnp.int32, (TM_IN, HD), 1)
            swap_lo = (lane & 32) == 0
            for ch in chunks:
                cols = slice(ch * HD, (ch + 1) * HD)
                o_ref[:, cols] = _rope(acc[:, cols], cos_ref[...], sin_ref[...], swap_lo).astype(BF16)


def _inproj(l, xall, norm1, mods5, w_in, cos, sin_signed):
    tiles_per_seq = S // TM_IN
    return pl.pallas_call(
        _inproj_kernel,
        grid=(T_ALL // TM_IN, IN_COLS // TN_IN),
        in_specs=[
            pl.BlockSpec((TM_IN, D), lambda i, j: (i, 0)),
            pl.BlockSpec((None, 1, D), lambda i, j: (l, 0, 0)),
            _mod_spec(l, 0, TM_IN),
            _mod_spec(l, 1, TM_IN),
            pl.BlockSpec((None, D, TN_IN), lambda i, j: (l, 0, j)),
            pl.BlockSpec((TM_IN, HD), lambda i, j: (i % tiles_per_seq, 0)),
            pl.BlockSpec((TM_IN, HD), lambda i, j: (i % tiles_per_seq, 0)),
        ],
        out_specs=pl.BlockSpec((TM_IN, TN_IN), lambda i, j: (i, j)),
        out_shape=jax.ShapeDtypeStruct((T_ALL, IN_COLS), BF16),
        scratch_shapes=[pltpu.VMEM((TM_IN, D), BF16)],
        compiler_params=_cparams("arbitrary", "arbitrary"),
        name="inproj",
    )(xall, norm1.reshape(DEPTH, 1, D), mods5, mods5, w_in, cos, sin_signed)


A_KEYS = 3 * A_BLOCK


def _stack_heads(q, rep):
    if rep == 1:
        return q
    return jnp.concatenate([q[:, r * HD:(r + 1) * HD] for r in range(rep)], axis=0)


def _dot_nt(a, b):
    return lax.dot_general(a, b, (((1,), (1,)), ((), ())), preferred_element_type=F32)


def _attn_a_kernel(sink_ref, q_ref, k_ref, v_ref, kc_ref, vc_ref, o_ref):
    n = pl.program_id(1)
    start = pl.multiple_of(jnp.clip(n * A_BLOCK - A_BLOCK, 0, S - A_KEYS), A_BLOCK)
    rows = A_REP * A_BLOCK
    qw = A_REP * HD
    qpos = n * A_BLOCK + (lax.broadcasted_iota(jnp.int32, (rows, A_KEYS), 0) & (A_BLOCK - 1))
    kpos = start + lax.broadcasted_iota(jnp.int32, (rows, A_KEYS), 1)
    in_window = jnp.abs(kpos - qpos) <= A_WIN
    ridx = lax.broadcasted_iota(jnp.int32, (rows, 1), 0)
    groups = range(A_KV)
    kv = [slice(g * HD, (g + 1) * HD) for g in groups]

    q4 = [_stack_heads(q_ref[:, g * qw:(g + 1) * qw], A_REP) for g in groups]
    s_loc = [jnp.where(in_window, _dot_nt(q4[g], k_ref[pl.ds(start, A_KEYS), kv[g]]) * SCALE, NEG)
             for g in groups]
    s_ctx = [_dot_nt(q4[g], kc_ref[:, kv[g]]) * SCALE for g in groups]
    e_loc, e_ctx, denom = [], [], []
    for g in groups:
        sink = jnp.full((rows, 1), sink_ref[g * A_REP], F32)
        for r in range(1, A_REP):
            sink = jnp.where(ridx >= r * A_BLOCK, sink_ref[g * A_REP + r], sink)
        m = jnp.maximum(jnp.maximum(jnp.max(s_loc[g], axis=1, keepdims=True),
                                    jnp.max(s_ctx[g], axis=1, keepdims=True)), sink)
        e_loc.append(jnp.exp(s_loc[g] - m))
        e_ctx.append(jnp.exp(s_ctx[g] - m))
        denom.append(jnp.sum(e_loc[g], axis=1, keepdims=True) + jnp.sum(e_ctx[g], axis=1, keepdims=True)
                     + jnp.exp(sink - m))
    for g in groups:
        o = (jnp.dot(e_loc[g].astype(BF16), v_ref[pl.ds(start, A_KEYS), kv[g]], preferred_element_type=F32)
             + jnp.dot(e_ctx[g].astype(BF16), vc_ref[:, kv[g]], preferred_element_type=F32))
        o = o / denom[g]
        for r in range(A_REP):
            h = g * A_REP + r
            o_ref[:, h * HD:(h + 1) * HD] = o[r * A_BLOCK:(r + 1) * A_BLOCK, :].astype(BF16)


def _attn_a(p, sink_l):
    nblk = S // A_BLOCK
    aw = A_HEADS * HD
    kvw = A_KV * HD
    return pl.pallas_call(
        _attn_a_kernel,
        grid_spec=pltpu.PrefetchScalarGridSpec(
            num_scalar_prefetch=1,
            grid=(NB, nblk),
            in_specs=[
                pl.BlockSpec((A_BLOCK, aw), lambda b, n, s: (b * nblk + n, OFF_QA // aw)),
                pl.BlockSpec((S, kvw), lambda b, n, s: (b, OFF_KA // kvw)),
                pl.BlockSpec((S, kvw), lambda b, n, s: (b, OFF_VA // kvw)),
                pl.BlockSpec((CTX, kvw), lambda b, n, s: (T_LAT // CTX + b, OFF_KA // kvw)),
                pl.BlockSpec((CTX, kvw), lambda b, n, s: (T_LAT // CTX + b, OFF_VA // kvw)),
            ],
            out_specs=pl.BlockSpec((A_BLOCK, aw), lambda b, n, s: (b * nblk + n, 0)),
        ),
        out_shape=jax.ShapeDtypeStruct((T_LAT, aw), BF16),
        compiler_params=_cparams("arbitrary", "arbitrary"),
        name="attn_a",
    )(sink_l, p, p, p, p, p)


B_GROUP = 8
B_Q = B_GROUP * GRID_W
B_KROWS = 16
B_K = B_KROWS * GRID_W
B_NTAB = 2 * B_WIN_ROWS - 2


def _bias_tables(rpb):
    c = np.arange(GRID_W)[:, None]
    kc = np.arange(GRID_W)[None, :]
    cs = np.clip(c - B_WIN_COLS // 2, 0, GRID_W - B_WIN_COLS)
    ok = (kc >= cs) & (kc < cs + B_WIN_COLS)
    idx = np.clip(kc - c, 1 - B_WIN_COLS, B_WIN_COLS - 1) + B_WIN_COLS - 1
    mt = jnp.where(jnp.asarray(ok), rpb[..., idx].astype(F32), NEG)
    return jnp.concatenate([mt[:, :, :-1], mt[:, :, 1:]], axis=-1)


B_HPS = 4


def _attn_b_kernel(q_ref, k_ref, v_ref, kc_ref, vc_ref, t_ref, o_ref, s_ref, p_ref, oloc_ref, pc_ref, inv_ref):
    g = pl.program_id(2)
    n_groups = (S // GRID_W) // B_GROUP
    r0 = g * B_GROUP
    base = jnp.clip(r0 - B_WIN_ROWS // 2, 0, S // GRID_W - B_KROWS)
    kstart = base * GRID_W
    win = B_WIN_ROWS * GRID_W
    heads = [slice(hh * HD, (hh + 1) * HD) for hh in range(B_HPS)]
    sc_all = [_dot_nt(q_ref[:, hc], kc_ref[:, hc]) * SCALE for hc in heads]

    def body(off, jlo_of):
        def window(ref, i, hc):
            k0 = pl.multiple_of(kstart + jlo_of(i) * GRID_W, GRID_W)
            return ref[pl.ds(k0, win), hc]

        for hh, hc in enumerate(heads):
            for i in range(B_GROUP):
                rows = slice(i * GRID_W, (i + 1) * GRID_W)
                s_ref[hh, rows, :] = _dot_nt(q_ref[rows, hc], window(k_ref, i, hc))
        for hh, hc in enumerate(heads):
            for i in range(B_GROUP):
                rows = slice(i * GRID_W, (i + 1) * GRID_W)
                bias = jnp.concatenate(
                    [t_ref[hh, off + jlo_of(i) + 2 * pp - i + B_WIN_ROWS - 1] for pp in range(B_WIN_ROWS // 2)],
                    axis=1)
                sl = s_ref[hh, rows, :] * SCALE + bias
                sc = sc_all[hh][rows, :]
                m = jnp.maximum(jnp.max(sl, axis=1, keepdims=True), jnp.max(sc, axis=1, keepdims=True))
                e = jnp.exp(sl - m)
                ec = jnp.exp(sc - m)
                denom = jnp.sum(e, axis=1, keepdims=True) + jnp.sum(ec, axis=1, keepdims=True)
                p_ref[hh, rows, :] = e.astype(BF16)
                pc_ref[hh, rows, :] = ec.astype(BF16)
                inv_ref[hh, rows, :] = 1.0 / denom
        for hh, hc in enumerate(heads):
            for i in range(B_GROUP):
                rows = slice(i * GRID_W, (i + 1) * GRID_W)
                oloc_ref[hh, rows, :] = jnp.dot(p_ref[hh, rows, :], window(v_ref, i, hc),
                                                preferred_element_type=F32)

    half = B_WIN_ROWS // 2

    @pl.when(g == 0)
    def _():
        body(0, lambda i: max(i - half, 0))

    @pl.when(jnp.logical_and(g > 0, g < n_groups - 1))
    def _():
        body(-half, lambda i: i)

    @pl.when(g == n_groups - 1)
    def _():
        body(-(B_KROWS - B_GROUP), lambda i: min(i + half, B_KROWS - B_WIN_ROWS))

    for hh, hc in enumerate(heads):
        o = oloc_ref[hh] + jnp.dot(pc_ref[hh], vc_ref[:, hc], preferred_element_type=F32)
        o_ref[:, hc] = (o * inv_ref[hh]).astype(BF16)


def _attn_b(l, p, tables):
    n_groups = (S // GRID_W) // B_GROUP
    hw = B_HPS * HD
    return pl.pallas_call(
        _attn_b_kernel,
        grid=(B_HEADS // B_HPS, NB, n_groups),
        in_specs=[
            pl.BlockSpec((B_Q, hw), lambda h, b, g: (b * n_groups + g, OFF_QB // hw + h)),
            pl.BlockSpec((S, hw), lambda h, b, g: (b, OFF_KB // hw + h)),
            pl.BlockSpec((S, hw), lambda h, b, g: (b, OFF_VB // hw + h)),
            pl.BlockSpec((CTX, hw), lambda h, b, g: (T_LAT // CTX + b, OFF_KB // hw + h)),
            pl.BlockSpec((CTX, hw), lambda h, b, g: (T_LAT // CTX + b, OFF_VB // hw + h)),
            pl.BlockSpec((None, B_HPS, B_NTAB, GRID_W, 2 * GRID_W), lambda h, b, g: (l, h, 0, 0, 0)),
        ],
        out_specs=pl.BlockSpec((B_Q, hw), lambda h, b, g: (b * n_groups + g, h)),
        out_shape=jax.ShapeDtypeStruct((T_LAT, B_HEADS * HD), BF16),
        scratch_shapes=[
            pltpu.VMEM((B_HPS, B_Q, B_WIN_ROWS * GRID_W), F32),
            pltpu.VMEM((B_HPS, B_Q, B_WIN_ROWS * GRID_W), BF16),
            pltpu.VMEM((B_HPS, B_Q, HD), F32),
            pltpu.VMEM((B_HPS, B_Q, CTX), BF16),
            pltpu.VMEM((B_HPS, B_Q, 1), F32),
        ],
        compiler_params=_cparams("arbitrary", "arbitrary", "arbitrary"),
        name="attn_b",
    )(p, p, p, p, p, tables)


def _ctx_attn_kernel(sink_ref, q_ref, k_ref, v_ref, o_ref, *, rep, has_sink):
    g = pl.program_id(1)
    qs = _stack_heads(q_ref[...], rep)
    s = _dot_nt(qs, k_ref[...]) * SCALE
    m = jnp.max(s, axis=1, keepdims=True)
    rows = rep * CTX
    if has_sink:
        ridx = lax.broadcasted_iota(jnp.int32, (rows, 1), 0)
        sink = jnp.full((rows, 1), sink_ref[g * rep], F32)
        for r in range(1, rep):
            sink = jnp.where(ridx >= r * CTX, sink_ref[g * rep + r], sink)
        m = jnp.maximum(m, sink)
    e = jnp.exp(s - m)
    denom = jnp.sum(e, axis=1, keepdims=True)
    if has_sink:
        denom = denom + jnp.exp(sink - m)
    o = jnp.dot(e.astype(BF16), v_ref[...], preferred_element_type=F32) / denom
    for r in range(rep):
        o_ref[:, r * HD:(r + 1) * HD] = o[r * CTX:(r + 1) * CTX, :].astype(BF16)


def _ctx_attn(p, sink_l, *, off_q, off_k, off_v, n_kv, rep, has_sink):
    qw = rep * HD
    rb = T_LAT // CTX
    kern = functools.partial(_ctx_attn_kernel, rep=rep, has_sink=has_sink)
    return pl.pallas_call(
        kern,
        grid_spec=pltpu.PrefetchScalarGridSpec(
            num_scalar_prefetch=1,
            grid=(NB, n_kv),
            in_specs=[
                pl.BlockSpec((CTX, qw), lambda b, g, s: (rb + b, off_q // qw + g)),
                pl.BlockSpec((CTX, HD), lambda b, g, s: (rb + b, off_k // HD + g)),
                pl.BlockSpec((CTX, HD), lambda b, g, s: (rb + b, off_v // HD + g)),
            ],
            out_specs=pl.BlockSpec((CTX, qw), lambda b, g, s: (b, g)),
        ),
        out_shape=jax.ShapeDtypeStruct((T_CTX, n_kv * qw), BF16),
        compiler_params=_cparams("arbitrary", "arbitrary"),
        name="ctx_attn_a" if has_sink else "ctx_attn_b",
    )(sink_l, p, p, p)


TM_M = 256
Z_COLS = OFF_VB - OFF_ZA


def _mix_kernel(*refs, has_ctx):
    if has_ctx:
        a_ref, b_ref, ac_ref, bc_ref = refs[:4]
        refs = refs[4:]
    else:
        a_ref, b_ref = refs[:2]
        refs = refs[2:]
    c_ref, cp_ref, cn_ref, z_ref, x_ref, gate_ref, wa_ref, wb_ref, wc_ref, wo_ref, cw_ref, o_ref = refs
    i = pl.program_id(0)
    is_lat = i < T_LAT // TM_M
    if has_ctx:
        a = jnp.where(is_lat, a_ref[...], ac_ref[...])
        b = jnp.where(is_lat, b_ref[...], bc_ref[...])
    else:
        a, b = a_ref[...], b_ref[...]

    w = C_WIDTH

    def gated(ref):
        return ref[:, w:2 * w].astype(F32) * ref[:, 0:w].astype(F32)

    v = gated(c_ref)
    vp = gated(cp_ref)[HALO - 1:HALO, :]
    vn = gated(cn_ref)[0:1, :]
    row = lax.broadcasted_iota(jnp.int32, (TM_M, w), 0)
    v_prev = jnp.where(row == 0, vp, pltpu.roll(v, 1, axis=0))
    v_next = jnp.where(row == TM_M - 1, vn, pltpu.roll(v, TM_M - 1, axis=0))
    lmask = jnp.where(is_lat, S - 1, CTX - 1)
    pos = (i * TM_M + row) & lmask
    cw = cw_ref[...]
    conv = (cw[0:1, :] * jnp.where(pos != 0, v_prev, 0.0) + cw[1:2, :] * v
            + cw[2:3, :] * jnp.where(pos != lmask, v_next, 0.0))
    cb = (c_ref[:, 2 * w:3 * w].astype(F32) * conv).astype(BF16)

    ya = jnp.dot(a, wa_ref[...], preferred_element_type=F32)
    yb = jnp.dot(b, wb_ref[...], preferred_element_type=F32)
    yc = jnp.dot(cb, wc_ref[...], preferred_element_type=F32)
    m = (jax.nn.sigmoid(z_ref[:, 0:D].astype(F32)) * ya + jax.nn.sigmoid(z_ref[:, D:2 * D].astype(F32)) * yb
         + jax.nn.sigmoid(z_ref[:, 2 * D:3 * D].astype(F32)) * yc)
    y = jnp.dot(m.astype(BF16), wo_ref[...], preferred_element_type=F32)
    o_ref[...] = x_ref[...] + gate_ref[...] * y


def _mix(l, with_ctx, attn_lat, attn_ctx, p, xall, mods5, w_pa, w_pb, w_pc, w_o, conv_c):
    cw3 = 3 * C_WIDTH
    hb = TM_M // HALO
    last_hb = T_ALL // HALO - 1
    lat_tiles = T_LAT // TM_M
    n_tiles = (T_ALL if with_ctx else T_LAT) // TM_M
    aw, bw = A_HEADS * HD, B_HEADS * HD
    attn_specs = [pl.BlockSpec((TM_M, aw), lambda i: (jnp.minimum(i, lat_tiles - 1), 0)),
                  pl.BlockSpec((TM_M, bw), lambda i: (jnp.minimum(i, lat_tiles - 1), 0))]
    attn_args = list(attn_lat)
    if with_ctx:
        attn_specs += [pl.BlockSpec((TM_M, aw), lambda i: (jnp.maximum(i - lat_tiles, 0), 0)),
                       pl.BlockSpec((TM_M, bw), lambda i: (jnp.maximum(i - lat_tiles, 0), 0))]
        attn_args += list(attn_ctx)

    def resident(shape):
        return pl.BlockSpec((None,) + shape, lambda i: (l, 0, 0), pipeline_mode=pl.Buffered(1))

    return pl.pallas_call(
        functools.partial(_mix_kernel, has_ctx=with_ctx),
        grid=(n_tiles,),
        in_specs=attn_specs + [
            pl.BlockSpec((TM_M, cw3), lambda i: (i, OFF_UC // cw3)),
            pl.BlockSpec((HALO, cw3), lambda i: (jnp.maximum(i * hb - 1, 0), OFF_UC // cw3)),
            pl.BlockSpec((HALO, cw3), lambda i: (jnp.minimum((i + 1) * hb, last_hb), OFF_UC // cw3)),
            pl.BlockSpec((TM_M, Z_COLS), lambda i: (i, OFF_ZA // Z_COLS)),
            pl.BlockSpec((TM_M, D), lambda i: (i, 0)),
            pl.BlockSpec((None, None, None, 1, D), lambda i: (l, _mod_row(i, TM_M), 2, 0, 0)),
            resident((aw, D)),
            resident((bw, D)),
            resident((C_WIDTH, D)),
            resident((D, D)),
            pl.BlockSpec((None, 3, C_WIDTH), lambda i: (l, 0, 0)),
        ],
        out_specs=pl.BlockSpec((TM_M, D), lambda i: (i, 0)),
        out_shape=jax.ShapeDtypeStruct((n_tiles * TM_M, D), F32),
        compiler_params=_cparams("arbitrary"),
        name="mix",
    )(*attn_args, p, p, p, p, xall, mods5, w_pa, w_pb, w_pc, w_o, conv_c)


TM_F = 1024
CH = 256
N_CH = D_FF // CH
ROWS_PER_PIECE = 64


def _chunk_major(w):
    r = w.shape[1]
    w = w.reshape(DEPTH, r, 2, N_CH, CH)
    return jnp.transpose(w, (0, 3, 1, 2, 4)).reshape(DEPTH, N_CH, r, 2 * CH)


def _chunk_up_kernel(g_ref, v_ref, o_ref):
    o_ref[:, 0:CH] = g_ref[...].astype(BF16)
    o_ref[:, CH:2 * CH] = v_ref[...].astype(BF16)


def _chunk_up(w_up):
    return pl.pallas_call(
        _chunk_up_kernel,
        grid=(DEPTH, N_CH),
        in_specs=[pl.BlockSpec((None, D, CH), lambda l, k: (l, 0, k)),
                  pl.BlockSpec((None, D, CH), lambda l, k: (l, 0, N_CH + k))],
        out_specs=pl.BlockSpec((None, None, D, 2 * CH), lambda l, k: (l, k, 0, 0)),
        out_shape=jax.ShapeDtypeStruct((DEPTH, N_CH, D, 2 * CH), BF16),
        compiler_params=_cparams("arbitrary", "arbitrary"),
        name="chunk_up",
    )(w_up, w_up)


def _ffn_kernel(x_ref, xp_ref, xn_ref, g_ref, shift_ref, scale_ref, gate_ref,
                wu_ref, cw_ref, wd_ref, nf_ref, o_ref, h_ref, act0_ref, act1_ref, u0_ref, u1_ref,
                *, final_norm, lat_tiles):
    i = pl.program_id(0)
    j = pl.program_id(1)
    nj = N_CH
    n_slab = 2 * CH // HD
    half = n_slab // 2

    @pl.when(j == 0)
    def _():
        g, sh, sc = g_ref[...], shift_ref[...], scale_ref[...]
        h_ref[0:HALO, :] = _norm_mod(xp_ref[...], g, sh, sc).astype(BF16)
        h_ref[HALO:HALO + TM_F, :] = _norm_mod(x_ref[...], g, sh, sc).astype(BF16)
        h_ref[HALO + TM_F:, :] = _norm_mod(xn_ref[...], g, sh, sc).astype(BF16)
        o_ref[...] = jnp.zeros((TM_F, D), F32)

    act_ref = (act0_ref, act1_ref)
    u_ref = (u0_ref, u1_ref)

    def up(slot):
        u = jnp.dot(h_ref[...], wu_ref[...], preferred_element_type=F32)
        for c in range(n_slab):
            u_ref[slot][c] = u[:, c * HD:(c + 1) * HD]

    def epilogue(slot):
        rb = ROWS_PER_PIECE
        lmask = jnp.where(i < lat_tiles, S - 1, CTX - 1)

        def conv(c, r0, has_prev, has_next):
            u_prev = jnp.where(has_prev, u_ref[slot][c, HALO - 1 + r0:HALO - 1 + r0 + rb, :], 0.0)
            u_next = jnp.where(has_next, u_ref[slot][c, HALO + 1 + r0:HALO + 1 + r0 + rb, :], 0.0)
            cw = cw_ref[:, c * HD:(c + 1) * HD]
            return (cw[0:1, :] * u_prev + cw[1:2, :] * u_ref[slot][c, HALO + r0:HALO + r0 + rb, :]
                    + cw[2:3, :] * u_next)

        for r0 in range(0, TM_F, rb):
            pos = (i * TM_F + r0 + lax.broadcasted_iota(jnp.int32, (rb, HD), 0)) & lmask
            has_prev, has_next = pos != 0, pos != lmask
            for c in range(half):
                cgate = conv(c, r0, has_prev, has_next)
                cval = conv(half + c, r0, has_prev, has_next)
                act_ref[slot][r0:r0 + rb, c * HD:(c + 1) * HD] = (
                    cgate * jax.nn.sigmoid(cgate) * cval).astype(BF16)

    def down(slot):
        return jnp.dot(act_ref[slot][...], wd_ref[...], preferred_element_type=F32)

    @pl.when(j == 0)
    def _():
        up(0)

    @pl.when(j == 1)
    def _():
        epilogue(0)
        up(1)

    for parity in (0, 1):
        @pl.when(jnp.logical_and(jnp.logical_and(j >= 2, j < nj), j % 2 == parity))
        def _():
            epilogue(1 - parity)
            up(parity)
            o_ref[...] += down(parity)

    @pl.when(j == nj)
    def _():
        epilogue((nj - 1) % 2)
        o_ref[...] += down(nj % 2)

    @pl.when(j == nj + 1)
    def _():
        xo = x_ref[...] + gate_ref[...] * (o_ref[...] + down((nj - 1) % 2))
        if final_norm:
            ms = jnp.mean(xo * xo, axis=-1, keepdims=True)
            xo = xo * lax.rsqrt(ms + EPS) * nf_ref[...]
        o_ref[...] = xo


def _ffn(l, with_ctx, xall, norm2, mods5, w_up_c, conv_f_c, w_down, norm_f, *, final_norm):
    hb = TM_F // HALO
    last_hb = xall.shape[0] // HALO - 1
    lat_tiles = T_LAT // TM_F
    n_tiles = (T_ALL if with_ctx else T_LAT) // TM_F
    kern = functools.partial(_ffn_kernel, final_norm=final_norm, lat_tiles=lat_tiles)

    def chunk(j, lag):
        return jnp.clip(j - lag, 0, N_CH - 1)

    return pl.pallas_call(
        kern,
        grid=(n_tiles, N_CH + 2),
        in_specs=[
            pl.BlockSpec((TM_F, D), lambda i, j: (i, 0)),
            pl.BlockSpec((HALO, D), lambda i, j: (jnp.maximum(i * hb - 1, 0), 0)),
            pl.BlockSpec((HALO, D), lambda i, j: (jnp.minimum((i + 1) * hb, last_hb), 0)),
            pl.BlockSpec((None, 1, D), lambda i, j: (l, 0, 0)),
            _mod_spec(l, 3, TM_F),
            _mod_spec(l, 4, TM_F),
            _mod_spec(l, 5, TM_F),
            pl.BlockSpec((None, None, D, 2 * CH), lambda i, j: (l, chunk(j, 0), 0, 0)),
            pl.BlockSpec((None, None, 3, 2 * CH), lambda i, j: (l, chunk(j, 1), 0, 0)),
            pl.BlockSpec((None, CH, D), lambda i, j: (l, chunk(j, 2), 0)),
            pl.BlockSpec((1, D), lambda i, j: (0, 0)),
        ],
        out_specs=pl.BlockSpec((TM_F, D), lambda i, j: (i, 0)),
        out_shape=jax.ShapeDtypeStruct((n_tiles * TM_F, D), F32),
        scratch_shapes=([pltpu.VMEM((TM_F + 2 * HALO, D), BF16)] + 2 * [pltpu.VMEM((TM_F, CH), BF16)]
                        + 2 * [pltpu.VMEM((2 * CH // HD, TM_F + 2 * HALO, HD), F32)]),
        compiler_params=_cparams("arbitrary", "arbitrary"),
        name="ffn_final" if final_norm else "ffn",
    )(xall, xall, xall, norm2.reshape(DEPTH, 1, D), mods5, mods5, mods5,
      w_up_c, conv_f_c, w_down, norm_f.reshape(1, D))


def _rope_tables():
    t = jnp.arange(S)
    row = (t // GRID_W).astype(F32)
    col = (t % GRID_W).astype(F32)
    n_freq = HD // 4
    inv = 1.0 / (ROPE_THETA ** (jnp.arange(n_freq, dtype=F32) / n_freq))
    ar = row[:, None] * inv
    ac = col[:, None] * inv
    ang = jnp.concatenate([ar, ar, ac, ac], axis=-1)
    lane = jnp.arange(HD)
    sign = jnp.where((lane & 32) == 0, -1.0, 1.0).astype(F32)
    return jnp.cos(ang), jnp.sin(ang) * sign


def _permute_in_cols(w):
    return jnp.concatenate([
        w[..., 0:1024],
        w[..., 1536:2560],
        w[..., 2560:3584],
        w[..., 4608:7680],
        w[..., 7680:13824],
        w[..., 3584:4608],
        w[..., 1024:1536],
    ], axis=-1)


def kernel(x, c, ctx, c_ctx, w_mod, b_mod, norm1, w_in, sink, rpb, conv_c, w_pa, w_pb, w_pc, w_o,
           norm2, w_up, conv_f, w_down, norm_f):
    c8 = jnp.concatenate([c, c_ctx[None, :], jnp.zeros((8 - NB - 1, D), F32)], axis=0)
    mods = _mods(c8, w_mod, b_mod)
    mods5 = mods.reshape(DEPTH, 8, N_MOD, 1, D)
    cos, sin_signed = _rope_tables()
    tables = _bias_tables(rpb)
    w_in_b = _permute_in_cols(w_in).astype(BF16)
    w_pa_b, w_pb_b, w_pc_b = w_pa.astype(BF16), w_pb.astype(BF16), w_pc.astype(BF16)
    w_o_b, w_down_b = w_o.astype(BF16), w_down.astype(BF16)
    w_up_c, conv_f_c = _chunk_up(w_up), _chunk_major(conv_f)

    xall = jnp.concatenate([x.reshape(T_LAT, D), ctx.reshape(T_CTX, D)], axis=0)
    for l in range(DEPTH):
        last = l == DEPTH - 1
        p = _inproj(l, xall, norm1, mods5, w_in_b, cos, sin_signed)
        attn_lat = (_attn_a(p, sink[l]), _attn_b(l, p, tables))
        attn_ctx = None
        if not last:
            attn_ctx = (_ctx_attn(p, sink[l], off_q=OFF_QA, off_k=OFF_KA, off_v=OFF_VA,
                                  n_kv=A_KV, rep=A_REP, has_sink=True),
                        _ctx_attn(p, sink[l], off_q=OFF_QB, off_k=OFF_KB, off_v=OFF_VB,
                                  n_kv=B_HEADS, rep=1, has_sink=False))
        xall = _mix(l, not last, attn_lat, attn_ctx, p, xall, mods5, w_pa_b, w_pb_b, w_pc_b, w_o_b, conv_c)
        xall = _ffn(l, not last, xall, norm2, mods5, w_up_c, conv_f_c, w_down_b, norm_f, final_norm=last)
    return xall.reshape(NB, S, D)
```
